```python
import jax, jax.numpy as jnp
from jax import lax
import numpy as np

D_MODEL = 1024
BATCH = 32
SEQ = 256
DEPTH = 2
DEC_BATCH = 4
DEC_SEQ = 2048
PAST_LEN = 512

GRID_W = 64
MIX_W = D_MODEL
FOURIER_W = MIX_W // 2
FOURIER_HEADS = 4
FOURIER_HD = FOURIER_W // FOURIER_HEADS
GLA_DV_W = MIX_W - FOURIER_W
GLA_DK_W = GLA_DV_W // 2
GLA_HEADS = 4
DV = GLA_DV_W // GLA_HEADS
DK = GLA_DK_W // GLA_HEADS
GATE_RANK = 16
GATE_TEMP = 16.0
CHUNK = 64
D_FF = 11 * D_MODEL // 4
EPS = 1e-6
POS_BASE = 10000.0
S_F = FOURIER_W
S_Q = S_F + GLA_DK_W
S_K = S_Q + GLA_DK_W
S_V = S_K + GLA_DV_W
S_AF = S_V + GATE_RANK
S_AB = S_AF + GATE_RANK
IN_COLS = S_AB + GLA_DV_W

kernel_name = "hymba_fnet_gla_convffn_diffusion_step"


def rmsnorm(x, g):
    xf = x.astype(jnp.float32)
    y = xf * lax.rsqrt(jnp.mean(xf * xf, axis=-1, keepdims=True) + EPS)
    return (y * g.astype(jnp.float32)).astype(x.dtype)


def grid_pos_embed(rows, d):
    quarter = d // 4
    omega = 1.0 / (POS_BASE ** (jnp.arange(quarter, dtype=jnp.float32) / quarter))
    er = jnp.arange(rows, dtype=jnp.float32)[:, None] * omega
    ec = jnp.arange(GRID_W, dtype=jnp.float32)[:, None] * omega
    pr = jnp.concatenate([jnp.sin(er), jnp.cos(er)], axis=-1)
    pc = jnp.concatenate([jnp.sin(ec), jnp.cos(ec)], axis=-1)
    pe = jnp.concatenate([jnp.broadcast_to(pr[:, None], (rows, GRID_W, d // 2)),
                          jnp.broadcast_to(pc[None], (rows, GRID_W, d // 2))], axis=-1)
    return pe.reshape(rows * GRID_W, d)


def gla_chunk_scan(q, k, v, log_a, s0):
    B, H, T, _ = q.shape
    nc = T // CHUNK
    rs = lambda a: a.reshape(B, H, nc, CHUNK, a.shape[-1])
    q, k, v, log_a = rs(q), rs(k), rs(v), rs(log_a)
    b = jnp.cumsum(log_a, axis=3)
    b_last = b[:, :, :, -1:, :]
    q_t = q * jnp.exp(b)
    k_t = k * jnp.exp(-b)
    k_end = k * jnp.exp(b_last - b)
    mask = jnp.tril(jnp.ones((CHUNK, CHUNK), dtype=bool))
    att = jnp.where(mask, jnp.einsum('bhnid,bhnjd->bhnij', q_t, k_t), 0.0)
    o_intra = jnp.einsum('bhnij,bhnjv->bhniv', att, v)
    kv_chunk = jnp.einsum('bhnjd,bhnjv->bhndv', k_end, v)
    decay_chunk = jnp.exp(b_last[:, :, :, 0, :])

    def step(s, xs):
        dec, kv = xs
        return dec[..., None] * s + kv, s

    s_final, s_before = lax.scan(step, s0.astype(jnp.float32),
                                 (jnp.moveaxis(decay_chunk, 2, 0), jnp.moveaxis(kv_chunk, 2, 0)))
    s_before = jnp.moveaxis(s_before, 0, 2)
    o_inter = jnp.einsum('bhnid,bhndv->bhniv', q_t, s_before)
    return (o_intra + o_inter).reshape(B, H, T, DV), s_final


def token_mix(h, s0, lp):
    B, T, _ = h.shape
    z = h @ lp['w_in']
    zf, zq, zk, zv, zaf, zab, zg = jnp.split(
        z, (S_F, S_Q, S_K, S_V, S_AF, S_AB), axis=-1)
    zf = zf.reshape(B, T, FOURIER_HEADS, FOURIER_HD).astype(jnp.float32)
    yf = jnp.real(jnp.fft.fft2(zf, axes=(1, 3), norm='ortho')).reshape(B, T, FOURIER_W)
    heads = lambda a, d: a.astype(jnp.float32).reshape(B, T, GLA_HEADS, d).transpose(0, 2, 1, 3)
    q = heads(zq, DK) * (DK ** -0.5)
    k = heads(zk, DK)
    v = heads(zv, DV)
    la_f = heads(jax.nn.log_sigmoid((zaf @ lp['w_gate_f'] + lp['b_gate_f']).astype(jnp.float32)), DK) / GATE_TEMP
    la_b = heads(jax.nn.log_sigmoid((zab @ lp['w_gate_b'] + lp['b_gate_b']).astype(jnp.float32)), DK) / GATE_TEMP
    o_f, s_f = gla_chunk_scan(q, k, v, la_f, s0[:, 0])
    o_b, s_b = gla_chunk_scan(q[:, :, ::-1], k[:, :, ::-1], v[:, :, ::-1], la_b[:, :, ::-1], s0[:, 1])
    o = rmsnorm(o_f + o_b[:, :, ::-1], lp['g_gla'])
    o = o.transpose(0, 2, 1, 3).reshape(B, T, GLA_DV_W) * jax.nn.silu(zg.astype(jnp.float32))
    y = jnp.concatenate([yf, o], axis=-1).astype(h.dtype) @ lp['w_out']
    return y, jnp.stack([s_f, s_b], axis=1)


def conv_ffn(h, n_seg, lp):
    B, T, _ = h.shape
    u = (h @ lp['w_up']).reshape(B, n_seg, T // n_seg, 2 * D_FF)
    up = jnp.pad(u, ((0, 0), (0, 0), (1, 1), (0, 0)))
    cw = lp['conv_w']
    u = up[:, :, :-2] * cw[0] + up[:, :, 1:-1] * cw[1] + up[:, :, 2:] * cw[2] + lp['conv_b']
    val, gate = jnp.split(u.reshape(B, T, 2 * D_FF), 2, axis=-1)
    return (jax.nn.silu(gate) * val) @ lp['w_down']


def layer(x, mod, s0, n_seg, lp):
    shift_m, scale_m, gate_m, shift_f, scale_f, gate_f = jnp.split(mod, 6, axis=-1)
    h = rmsnorm(x, lp['g_pre_mix']) * (1.0 + scale_m) + shift_m
    o, s_fin = token_mix(h, s0, lp)
    x = x + gate_m * rmsnorm(o, lp['g_post_mix'])
    h = rmsnorm(x, lp['g_pre_ffn']) * (1.0 + scale_f) + shift_f
    x = x + gate_f * rmsnorm(conv_ffn(h, n_seg, lp), lp['g_post_ffn'])
    return x, s_fin


def setup_inputs(seed: int = 0) -> dict:
    key = jax.random.key(seed)
    ks = jax.random.split(key, 22)
    f32 = jnp.float32
    nrm = lambda k, shape, s: jax.random.normal(k, shape, f32) * s
    return {
        'x_prompt': nrm(ks[0], (BATCH, SEQ, D_MODEL), 1.0),
        'x_sample': nrm(ks[1], (DEC_BATCH, DEC_SEQ, D_MODEL), 1.0),
        'state_gla': nrm(ks[2], (DEC_BATCH, DEPTH, 2, GLA_HEADS, DK, DV), 1.0),
        'c': nrm(ks[3], (DEC_BATCH, D_MODEL), 1.0),
        'c_ctx': nrm(ks[4], (D_MODEL,), 1.0),
        'g_pre_mix': 1.0 + nrm(ks[5], (DEPTH, D_MODEL), 0.01),
        'g_post_mix': 1.0 + nrm(ks[6], (DEPTH, D_MODEL), 0.01),
        'g_pre_ffn': 1.0 + nrm(ks[7], (DEPTH, D_MODEL), 0.01),
        'g_post_ffn': 1.0 + nrm(ks[8], (DEPTH, D_MODEL), 0.01),
        'w_ada': nrm(ks[9], (DEPTH, D_MODEL, 6 * D_MODEL), 0.5 * D_MODEL ** -0.5),
        'b_ada': nrm(ks[10], (DEPTH, 6 * D_MODEL), 0.01),
        'w_in': nrm(ks[11], (DEPTH, D_MODEL, IN_COLS), D_MODEL ** -0.5),
        'w_gate_f': nrm(ks[12], (DEPTH, GATE_RANK, GLA_DK_W), GATE_RANK ** -0.5),
        'b_gate_f': nrm(ks[13], (DEPTH, GLA_DK_W), 0.5),
        'w_gate_b': nrm(ks[14], (DEPTH, GATE_RANK, GLA_DK_W), GATE_RANK ** -0.5),
        'b_gate_b': nrm(ks[15], (DEPTH, GLA_DK_W), 0.5),
        'g_gla': 1.0 + nrm(ks[16], (DEPTH, DV), 0.01),
        'w_out': nrm(ks[17], (DEPTH, MIX_W, D_MODEL), MIX_W ** -0.5),
        'w_up': nrm(ks[18], (DEPTH, D_MODEL, 2 * D_FF), D_MODEL ** -0.5),
        'conv_w': nrm(ks[19], (DEPTH, 3, 2 * D_FF), 3 ** -0.5),
        'conv_b': nrm(ks[20], (DEPTH, 2 * D_FF), 0.01),
        'w_down': nrm(ks[21], (DEPTH, D_FF, D_MODEL), D_FF ** -0.5),
    }


def reference(x_prompt, x_sample, state_gla, c, c_ctx, g_pre_mix, g_post_mix, g_pre_ffn,
              g_post_ffn, w_ada, b_ada, w_in, w_gate_f, b_gate_f, w_gate_b, b_gate_b, g_gla,
              w_out, w_up, conv_w, conv_b, w_down):
    def layer_params(l):
        return dict(g_pre_mix=g_pre_mix[l], g_post_mix=g_post_mix[l], g_pre_ffn=g_pre_ffn[l],
                    g_post_ffn=g_post_ffn[l], w_in=w_in[l], w_gate_f=w_gate_f[l],
                    b_gate_f=b_gate_f[l], w_gate_b=w_gate_b[l], b_gate_b=b_gate_b[l],
                    g_gla=g_gla[l], w_out=w_out[l], w_up=w_up[l], conv_w=conv_w[l],
                    conv_b=conv_b[l], w_down=w_down[l])

    xp = x_prompt
    s_zero = jnp.zeros((x_prompt.shape[0], 2, GLA_HEADS, DK, DV), jnp.float32)
    ctx_states = []
    for l in range(DEPTH):
        mod_ctx = (jax.nn.silu(c_ctx) @ w_ada[l] + b_ada[l])[None, None]
        xp, s_fin = layer(xp, mod_ctx, s_zero, 1, layer_params(l))
        ctx_states.append(s_fin)
    new_state_gla = jnp.stack(ctx_states, axis=1)

    rows = x_sample.shape[1] // GRID_W
    xs = x_sample + grid_pos_embed(rows, D_MODEL).astype(x_sample.dtype)[None]
    for l in range(DEPTH):
        mod = (jax.nn.silu(c) @ w_ada[l] + b_ada[l])[:, None]
        xs, _ = layer(xs, mod, state_gla[:, l], rows, layer_params(l))

    return (xp, xs, new_state_gla)
```

```python
import functools

import numpy as np
import jax
import jax.numpy as jnp
from jax import lax
from jax.experimental import pallas as pl
from jax.experimental.pallas import tpu as pltpu

D_MODEL = 1024
DEPTH = 2
GRID_W = 64
FOURIER_W = 512
FOURIER_HEADS = 4
FOURIER_HD = FOURIER_W // FOURIER_HEADS
GLA_DV_W = 512
GLA_DK_W = 256
GLA_HEADS = 4
DV = GLA_DV_W // GLA_HEADS
DK = GLA_DK_W // GLA_HEADS
GATE_RANK = 16
GATE_TEMP = 16.0
CHUNK = 64
D_FF = 11 * D_MODEL // 4
EPS = 1e-6
POS_BASE = 10000.0
S_F = FOURIER_W
S_Q = S_F + GLA_DK_W
S_K = S_Q + GLA_DK_W
S_V = S_K + GLA_DV_W
S_AF = S_V + GATE_RANK
S_AB = S_AF + GATE_RANK
IN_COLS = S_AB + GLA_DV_W

C_F = 0
C_Q = C_F + FOURIER_W
C_K = C_Q + GLA_DK_W
C_V = C_K + GLA_DK_W
C_G = C_V + GLA_DV_W
C_A = C_G + GLA_DV_W
LANE = 128
IN_COLS_PAD = C_A + LANE
MOD_ROWS = 8
BLK = 256

VMEM_LIMIT = 56 * 1024 * 1024

F32 = jnp.float32
BF16 = jnp.bfloat16


def _silu(x):
    return x * jax.nn.sigmoid(x)


def _rms(x, g):
    return x * lax.rsqrt(jnp.mean(x * x, axis=-1, keepdims=True) + EPS) * g


def _dot(a, b):
    return jnp.dot(a, b, preferred_element_type=F32)


def _dot_nt(a, b):
    return lax.dot_general(a, b, (((1,), (1,)), ((), ())), preferred_element_type=F32)


def _ada_kernel(cv_ref, w_ref, b_ref, o_ref):
    s = _silu(cv_ref[...]).astype(BF16)
    o_ref[0] = _dot(s, w_ref[0].astype(BF16)) + b_ref[0]


def _ada(cv, w_ada, b_ada):
    tn = 1536
    n_out = w_ada.shape[-1]
    return pl.pallas_call(
        _ada_kernel,
        grid=(DEPTH, n_out // tn),
        in_specs=[
            pl.BlockSpec((MOD_ROWS, D_MODEL), lambda l, n: (0, 0)),
            pl.BlockSpec((1, D_MODEL, tn), lambda l, n: (l, 0, n)),
            pl.BlockSpec((1, 1, tn), lambda l, n: (l, 0, n)),
        ],
        out_specs=pl.BlockSpec((1, MOD_ROWS, tn), lambda l, n: (l, 0, n)),
        out_shape=jax.ShapeDtypeStruct((DEPTH, MOD_ROWS, n_out), F32),
        compiler_params=pltpu.CompilerParams(
            dimension_semantics=("arbitrary", "arbitrary"), vmem_limit_bytes=VMEM_LIMIT),
        name="ada",
    )(cv, w_ada, b_ada.reshape(DEPTH, 1, n_out))


def _inproj_kernel(*refs, has_pe):
    if has_pe:
        x_ref, pe_ref, *refs = refs
    else:
        x_ref, *refs = refs
    (mod_ref, g_ref, w_ref, cc_ref, wg_ref, bg_ref,
     ab_ref, q_ref, k_ref, v_ref, laf_ref, lab_ref, sg_ref) = refs
    x = x_ref[...]
    if has_pe:
        x = x + pe_ref[...]
    mod = mod_ref[0]
    shift = mod[:, 0:D_MODEL]
    scale = mod[:, D_MODEL:2 * D_MODEL]
    h = (_rms(x, g_ref[...]) * (1.0 + scale) + shift).astype(BF16)
    z = _dot(h, w_ref[...])
    zf = z[:, C_F:C_Q].astype(BF16)
    for hh in range(FOURIER_HEADS):
        sl = slice(hh * FOURIER_HD, (hh + 1) * FOURIER_HD)
        cs = _dot(zf[:, sl], cc_ref[...])
        ab_ref[0, :, sl] = cs[:, :FOURIER_HD].astype(BF16)
        ab_ref[1, :, sl] = cs[:, FOURIER_HD:].astype(BF16)
    q_ref[...] = z[:, C_Q:C_K] * (DK ** -0.5)
    k_ref[...] = z[:, C_K:C_V]
    v_ref[...] = z[:, C_V:C_G].astype(BF16)
    sg_ref[...] = _silu(z[:, C_G:C_A]).astype(BF16)
    xg = _dot(z[:, C_A:IN_COLS_PAD].astype(BF16), wg_ref[...]) + bg_ref[...]
    la = (jnp.minimum(xg, 0.0) - jnp.log1p(jnp.exp(-jnp.abs(xg)))) * (1.0 / GATE_TEMP)
    laf_ref[...] = la[:, :GLA_DK_W]
    lab_ref[...] = la[:, GLA_DK_W:]


def _inproj(x, pe, mod, g, w_in_r, cc, wg, bg, seq_len, tm):
    n = x.shape[0]
    has_pe = pe is not None
    per_batch = mod.shape[0] > 1
    mod_idx = (lambda i: ((i * tm) // seq_len, 0, 0)) if per_batch else (lambda i: (0, 0, 0))
    row = lambda w: pl.BlockSpec((tm, w), lambda i: (i, 0))
    full = lambda a: pl.BlockSpec(a.shape, lambda i: (0,) * a.ndim)
    in_specs = [row(D_MODEL)]
    args = [x]
    if has_pe:
        in_specs.append(pl.BlockSpec((tm, D_MODEL), lambda i: (i % (seq_len // tm), 0)))
        args.append(pe)
    in_specs += [pl.BlockSpec((1, 1, 6 * D_MODEL), mod_idx), full(g), full(w_in_r), full(cc),
                 full(wg), full(bg)]
    args += [mod, g, w_in_r, cc, wg, bg]
    out_shape = (
        jax.ShapeDtypeStruct((2, n, FOURIER_W), BF16),
        jax.ShapeDtypeStruct((n, GLA_DK_W), F32),
        jax.ShapeDtypeStruct((n, GLA_DK_W), F32),
        jax.ShapeDtypeStruct((n, GLA_DV_W), BF16),
        jax.ShapeDtypeStruct((n, GLA_DK_W), F32),
        jax.ShapeDtypeStruct((n, GLA_DK_W), F32),
        jax.ShapeDtypeStruct((n, GLA_DV_W), BF16),
    )
    out_specs = (
        pl.BlockSpec((2, tm, FOURIER_W), lambda i: (0, i, 0)),
        row(GLA_DK_W), row(GLA_DK_W), row(GLA_DV_W), row(GLA_DK_W), row(GLA_DK_W), row(GLA_DV_W),
    )
    return pl.pallas_call(
        functools.partial(_inproj_kernel, has_pe=has_pe),
        grid=(n // tm,),
        in_specs=in_specs,
        out_specs=out_specs,
        out_shape=out_shape,
        compiler_params=pltpu.CompilerParams(
            dimension_semantics=("arbitrary",), vmem_limit_bytes=VMEM_LIMIT),
        name="inproj",
    )(*args)


def _seqdft_kernel(c_ref, s_ref, a_ref, b_ref, o_ref):
    o_ref[...] = (_dot(c_ref[...], a_ref[...]) - _dot(s_ref[...], b_ref[...])).astype(BF16)


def _seqdft(ab, ct, st, batch, seq_len, tr):
    ab4 = ab.reshape(2, batch, seq_len, FOURIER_W)
    return pl.pallas_call(
        _seqdft_kernel,
        grid=(batch, seq_len // tr),
        in_specs=[
            pl.BlockSpec((tr, seq_len), lambda b, r: (r, 0)),
            pl.BlockSpec((tr, seq_len), lambda b, r: (r, 0)),
            pl.BlockSpec((None, None, seq_len, FOURIER_W), lambda b, r: (0, b, 0, 0)),
            pl.BlockSpec((None, None, seq_len, FOURIER_W), lambda b, r: (1, b, 0, 0)),
        ],
        out_specs=pl.BlockSpec((None, tr, FOURIER_W), lambda b, r: (b, r, 0)),
        out_shape=jax.ShapeDtypeStruct((batch, seq_len, FOURIER_W), BF16),
        compiler_params=pltpu.CompilerParams(
            dimension_semantics=("arbitrary", "arbitrary"), vmem_limit_bytes=VMEM_LIMIT),
        name="seqdft",
    )(ct, st, ab4, ab4)


def _gla_kernel(*refs, seq_len, has_s0):
    if has_s0:
        s0_ref, *refs = refs
    (q_ref, k_ref, v_ref, laf_ref, lab_ref, sg_ref, g_ref,
     og_ref, sT_ref,
     qt_s, kt_s, ke_s, dec_s, o_s, st_s) = refs
    nc = seq_len // CHUNK
    la_refs = (laf_ref, lab_ref)

    ri = lax.broadcasted_iota(jnp.int32, (BLK, BLK), 0)
    ci = lax.broadcasted_iota(jnp.int32, (BLK, BLK), 1)
    same = (ri // CHUNK) == (ci // CHUNK)
    tri = (jnp.where(same & (ci <= ri), 1.0, 0.0).astype(BF16),
           jnp.where(same & (ci >= ri), 1.0, 0.0).astype(BF16))
    tot_m = jnp.where(same, 1.0, 0.0).astype(BF16)

    for blk in range(seq_len // BLK):
        rows = slice(blk * BLK, (blk + 1) * BLK)
        q = q_ref[rows, :]
        k = k_ref[rows, :]
        for d in range(2):
            la = la_refs[d][rows, :]
            hi = la.astype(BF16)
            lo = (la - hi.astype(F32)).astype(BF16)
            b = _dot(tri[d], hi) + _dot(tri[d], lo)
            tot = _dot(tot_m, hi) + _dot(tot_m, lo)
            qt_s[d, rows, :] = (q * jnp.exp(b)).astype(BF16)
            kt_s[d, rows, :] = (k * jnp.exp(-b)).astype(BF16)
            ke_s[d, rows, :] = (k * jnp.exp(tot - b)).astype(BF16)
            dec_s[d, rows, :] = jnp.exp(tot)
    o_s[...] = jnp.zeros_like(o_s)
    for d in range(2):
        if has_s0:
            st_s[d] = s0_ref[d]
        else:
            st_s[d] = jnp.zeros((DV, GLA_DK_W), F32)

    lane_head = lax.broadcasted_iota(jnp.int32, (1, GLA_DK_W), 1) // DK
    head_mask = [jnp.where(lane_head == h, 1.0, 0.0).astype(BF16) for h in range(GLA_HEADS)]
    ti = lax.broadcasted_iota(jnp.int32, (CHUNK, CHUNK), 0)
    tj = lax.broadcasted_iota(jnp.int32, (CHUNK, CHUNK), 1)
    causal = (tj <= ti, tj >= ti)

    def chunk_step(d, c):
        r0 = c * CHUNK
        if not isinstance(r0, int):
            r0 = pl.multiple_of(r0, CHUNK)
        rows = pl.ds(r0, CHUNK)
        qt_c = qt_s[d, rows, :]
        q_stack = jnp.concatenate([qt_c * head_mask[h] for h in range(GLA_HEADS)], axis=0)
        w = jnp.concatenate([st_s[d].astype(BF16), kt_s[d, rows, :]], axis=0)
        res = _dot_nt(q_stack, w)
        v_c = v_ref[rows, :]
        for h in range(GLA_HEADS):
            hr = slice(h * CHUNK, (h + 1) * CHUNK)
            hv = slice(h * DV, (h + 1) * DV)
            att = jnp.where(causal[d], res[hr, DV:DV + CHUNK], 0.0).astype(BF16)
            o_s[rows, hv] += res[hr, 0:DV] + _dot(att, v_c[:, hv])
        ke_c = ke_s[d, rows, :]
        k_bd = jnp.concatenate([ke_c * head_mask[h] for h in range(GLA_HEADS)], axis=0)
        v_stack = jnp.concatenate(
            [v_c[:, h * DV:(h + 1) * DV] for h in range(GLA_HEADS)], axis=0)
        v_stack_t = v_stack.astype(F32).T.astype(BF16)
        st_s[d] = dec_s[d, pl.ds(r0, 1), :] * st_s[d] + _dot(v_stack_t, k_bd)

    if nc <= 4:
        for i in range(nc):
            chunk_step(0, i)
            chunk_step(1, nc - 1 - i)
    else:
        def body(i, carry):
            chunk_step(0, i)
            chunk_step(1, nc - 1 - i)
            return carry
        lax.fori_loop(0, nc, body, 0)

    g = g_ref[...]
    for h in range(GLA_HEADS):
        hv = slice(h * DV, (h + 1) * DV)
        og_ref[:, hv] = (_rms(o_s[:, hv], g) * sg_ref[:, hv].astype(F32)).astype(BF16)
    sT_ref[...] = st_s[...]


def _gla(q, k, v, laf, lab, sg, g_gla, s0t, batch, seq_len):
    has_s0 = s0t is not None
    r3 = lambda a: a.reshape(batch, seq_len, a.shape[-1])
    seq = lambda w: pl.BlockSpec((None, seq_len, w), lambda b: (b, 0, 0))
    st_spec = pl.BlockSpec((None, 2, DV, GLA_DK_W), lambda b: (b, 0, 0, 0))
    in_specs = []
    args = []
    if has_s0:
        in_specs.append(st_spec)
        args.append(s0t)
    in_specs += [seq(GLA_DK_W), seq(GLA_DK_W), seq(GLA_DV_W), seq(GLA_DK_W), seq(GLA_DK_W),
                 seq(GLA_DV_W), pl.BlockSpec((1, DV), lambda b: (0, 0))]
    args += [r3(q), r3(k), r3(v), r3(laf), r3(lab), r3(sg), g_gla]
    return pl.pallas_call(
        functools.partial(_gla_kernel, seq_len=seq_len, has_s0=has_s0),
        grid=(batch,),
        in_specs=in_specs,
        out_specs=(seq(GLA_DV_W), st_spec),
        out_shape=(jax.ShapeDtypeStruct((batch, seq_len, GLA_DV_W), BF16),
                   jax.ShapeDtypeStruct((batch, 2, DV, GLA_DK_W), F32)),
        scratch_shapes=[
            pltpu.VMEM((2, seq_len, GLA_DK_W), BF16),
            pltpu.VMEM((2, seq_len, GLA_DK_W), BF16),
            pltpu.VMEM((2, seq_len, GLA_DK_W), BF16),
            pltpu.VMEM((2, seq_len, GLA_DK_W), F32),
            pltpu.VMEM((seq_len, GLA_DV_W), F32),
            pltpu.VMEM((2, DV, GLA_DK_W), F32),
        ],
        compiler_params=pltpu.CompilerParams(
            dimension_semantics=("arbitrary",), vmem_limit_bytes=VMEM_LIMIT),
        name="gla",
    )(*args)


def _tail_kernel(*refs, has_pe, seg, tm, tf, nj):
    if has_pe:
        x_ref, pe_ref, *refs = refs
    else:
        x_ref, *refs = refs
    (yf_ref, og_ref, mod_ref, wo_ref, gpm_ref, gpf_ref, gqf_ref,
     wup_ref, cw_ref, cb_ref, wdn_ref, out_ref, x1_s, h_s, acc_s) = refs
    j = pl.program_id(1)
    mod = mod_ref[0]

    @pl.when(j == 0)
    def _():
        x = x_ref[...]
        if has_pe:
            x = x + pe_ref[...]
        y = _dot(yf_ref[...], wo_ref[0:FOURIER_W, :]) + _dot(og_ref[...], wo_ref[FOURIER_W:, :])
        gate_m = mod[:, 2 * D_MODEL:3 * D_MODEL]
        x1 = x + gate_m * _rms(y, gpm_ref[...])
        x1_s[...] = x1
        shift_f = mod[:, 3 * D_MODEL:4 * D_MODEL]
        scale_f = mod[:, 4 * D_MODEL:5 * D_MODEL]
        h_s[...] = (_rms(x1, gpf_ref[...]) * (1.0 + scale_f) + shift_f).astype(BF16)
        acc_s[...] = jnp.zeros_like(acc_s)

    u = _dot(h_s[...], wup_ref[...])
    pos = lax.broadcasted_iota(jnp.int32, (tm, 1), 0) % seg
    u_prev = jnp.where(pos == 0, 0.0, pltpu.roll(u, 1, 0))
    u_next = jnp.where(pos == seg - 1, 0.0, pltpu.roll(u, tm - 1, 0))
    cw = cw_ref[...]
    uc = u_prev * cw[0:1] + u * cw[1:2] + u_next * cw[2:3] + cb_ref[...]
    a = (_silu(uc[:, tf:]) * uc[:, :tf]).astype(BF16)
    acc_s[...] += _dot(a, wdn_ref[...])

    @pl.when(j == nj - 1)
    def _():
        gate_f = mod[:, 5 * D_MODEL:6 * D_MODEL]
        out_ref[...] = x1_s[...] + gate_f * _rms(acc_s[...], gqf_ref[...])


def _tail(x, pe, yf, og, mod, w_out, gpm, gpf, gqf, w_up_r, cw_r, cb_r, w_down, seq_len, seg,
          tm, tf):
    n = x.shape[0]
    nj = D_FF // tf
    has_pe = pe is not None
    per_batch = mod.shape[0] > 1
    mod_idx = ((lambda i, j: ((i * tm) // seq_len, 0, 0)) if per_batch
               else (lambda i, j: (0, 0, 0)))
    row = lambda w: pl.BlockSpec((tm, w), lambda i, j: (i, 0))
    full = lambda a: pl.BlockSpec(a.shape, lambda i, j: (0,) * a.ndim)
    in_specs = [row(D_MODEL)]
    args = [x]
    if has_pe:
        in_specs.append(pl.BlockSpec((tm, D_MODEL), lambda i, j: (i % (seq_len // tm), 0)))
        args.append(pe)
    in_specs += [
        row(FOURIER_W), row(GLA_DV_W), pl.BlockSpec((1, 1, 6 * D_MODEL), mod_idx),
        full(w_out), full(gpm), full(gpf), full(gqf),
        pl.BlockSpec((None, D_MODEL, 2 * tf), lambda i, j: (j, 0, 0)),
        pl.BlockSpec((None, 3, 2 * tf), lambda i, j: (j, 0, 0)),
        pl.BlockSpec((None, 1, 2 * tf), lambda i, j: (j, 0, 0)),
        pl.BlockSpec((tf, D_MODEL), lambda i, j: (j, 0)),
    ]
    args += [yf, og, mod, w_out, gpm, gpf, gqf, w_up_r, cw_r, cb_r, w_down]
    return pl.pallas_call(
        functools.partial(_tail_kernel, has_pe=has_pe, seg=seg, tm=tm, tf=tf, nj=nj),
        grid=(n // tm, nj),
        in_specs=in_specs,
        out_specs=row(D_MODEL),
        out_shape=jax.ShapeDtypeStruct((n, D_MODEL), F32),
        scratch_shapes=[
            pltpu.VMEM((tm, D_MODEL), F32),
            pltpu.VMEM((tm, D_MODEL), BF16),
            pltpu.VMEM((tm, D_MODEL), F32),
        ],
        compiler_params=pltpu.CompilerParams(
            dimension_semantics=("arbitrary", "arbitrary"), vmem_limit_bytes=VMEM_LIMIT),
        name="tail",
    )(*args)


def _grid_pos_embed(rows, d):
    quarter = d // 4
    omega = 1.0 / (POS_BASE ** (jnp.arange(quarter, dtype=F32) / quarter))
    er = jnp.arange(rows, dtype=F32)[:, None] * omega
    ec = jnp.arange(GRID_W, dtype=F32)[:, None] * omega
    pr = jnp.concatenate([jnp.sin(er), jnp.cos(er)], axis=-1)
    pc = jnp.concatenate([jnp.sin(ec), jnp.cos(ec)], axis=-1)
    pe = jnp.concatenate([jnp.broadcast_to(pr[:, None], (rows, GRID_W, d // 2)),
                          jnp.broadcast_to(pc[None], (rows, GRID_W, d // 2))], axis=-1)
    return pe.reshape(rows * GRID_W, d)


def _dft_tables(n, scale):
    idx = jnp.arange(n, dtype=jnp.int32)
    ang = ((idx[:, None] * idx[None, :]) % n).astype(F32) * np.float32(2.0 * np.pi / n)
    return jnp.cos(ang) * np.float32(scale), jnp.sin(ang) * np.float32(scale)


def _layer_weights(l, w_in, w_gate_f, b_gate_f, w_gate_b, b_gate_b, w_out, w_up, conv_w, conv_b,
                   w_down, tf):
    nj = D_FF // tf
    wi = w_in[l]
    w_in_r = jnp.concatenate(
        [wi[:, :S_V], wi[:, S_AB:], wi[:, S_V:S_AB],
         jnp.zeros((D_MODEL, LANE - 2 * GATE_RANK), F32)], axis=1).astype(BF16)
    wg = jnp.zeros((LANE, 2 * GLA_DK_W), F32)
    wg = wg.at[0:GATE_RANK, 0:GLA_DK_W].set(w_gate_f[l])
    wg = wg.at[GATE_RANK:2 * GATE_RANK, GLA_DK_W:].set(w_gate_b[l])
    bg = jnp.concatenate([b_gate_f[l], b_gate_b[l]])[None]

    def interleave(a):
        lead = a.shape[:-1]
        val = a[..., :D_FF].reshape(lead + (nj, tf))
        gate = a[..., D_FF:].reshape(lead + (nj, tf))
        return jnp.moveaxis(jnp.concatenate([val, gate], axis=-1), -2, 0)

    return dict(
        w_in_r=w_in_r, wg=wg.astype(BF16), bg=bg, w_out=w_out[l].astype(BF16),
        w_up_r=interleave(w_up[l]).astype(BF16), cw_r=interleave(conv_w[l]),
        cb_r=interleave(conv_b[l][None]), w_down=w_down[l].astype(BF16))


def _stream_layer(x, pe, mod, s0t, lw, gains, dft, batch, seq_len, seg, tm_in, tr, tm_tail, tf):
    g_pre_mix, g_post_mix, g_pre_ffn, g_post_ffn, g_gla = gains
    cc, ct, st = dft
    ab, q, k, v, laf, lab, sg = _inproj(x, pe, mod, g_pre_mix, lw['w_in_r'], cc, lw['wg'],
                                        lw['bg'], seq_len, tm_in)
    yf = _seqdft(ab, ct, st, batch, seq_len, tr).reshape(batch * seq_len, FOURIER_W)
    og, s_t = _gla(q, k, v, laf, lab, sg, g_gla, s0t, batch, seq_len)
    x_new = _tail(x, pe, yf, og.reshape(batch * seq_len, GLA_DV_W), mod, lw['w_out'], g_post_mix,
                  g_pre_ffn, g_post_ffn, lw['w_up_r'], lw['cw_r'], lw['cb_r'], lw['w_down'],
                  seq_len, seg, tm_tail, tf)
    return x_new, s_t


def kernel(x_prompt, x_sample, state_gla, c, c_ctx, g_pre_mix, g_post_mix, g_pre_ffn, g_post_ffn,
           w_ada, b_ada, w_in, w_gate_f, b_gate_f, w_gate_b, b_gate_b, g_gla, w_out, w_up, conv_w,
           conv_b, w_down):
    pb, pt, _ = x_prompt.shape
    sb, s_t, _ = x_sample.shape
    rows = s_t // GRID_W
    tf = 256

    cv = jnp.concatenate([c_ctx[None], c, jnp.zeros((MOD_ROWS - 1 - sb, D_MODEL), F32)], axis=0)
    mod_all = _ada(cv, w_ada, b_ada)

    cc_c, cc_s = _dft_tables(FOURIER_HD, FOURIER_HD ** -0.5)
    cc = jnp.concatenate([cc_c, cc_s], axis=1).astype(BF16)
    dft_p = (cc,) + tuple(t.astype(BF16) for t in _dft_tables(pt, pt ** -0.5))
    dft_s = (cc,) + tuple(t.astype(BF16) for t in _dft_tables(s_t, s_t ** -0.5))
    pe = _grid_pos_embed(rows, D_MODEL)

    xp = x_prompt.reshape(pb * pt, D_MODEL)
    xs = x_sample.reshape(sb * s_t, D_MODEL)
    ctx_states = []
    for l in range(DEPTH):
        lw = _layer_weights(l, w_in, w_gate_f, b_gate_f, w_gate_b, b_gate_b, w_out, w_up, conv_w,
                            conv_b, w_down, tf)
        gains = (g_pre_mix[l][None], g_post_mix[l][None], g_pre_ffn[l][None],
                 g_post_ffn[l][None], g_gla[l][None])
        mod_p = mod_all[l, 0:1][:, None]
        mod_s = mod_all[l, 1:1 + sb][:, None]
        xp, st_p = _stream_layer(xp, None, mod_p, None, lw, gains, dft_p, pb, pt, pt,
                                 512, pt, 512, tf)
        ctx_states.append(
            st_p.reshape(pb, 2, DV, GLA_HEADS, DK).transpose(0, 1, 3, 4, 2))
        s0t = state_gla[:, l].transpose(0, 1, 4, 2, 3).reshape(sb, 2, DV, GLA_DK_W)
        xs, _ = _stream_layer(xs, pe if l == 0 else None, mod_s, s0t, lw, gains, dft_s, sb, s_t,
                              GRID_W, 512, 512, 512, tf)
    new_state = jnp.stack(ctx_states, axis=1)
    return (xp.reshape(pb, pt, D_MODEL), xs.reshape(sb, s_t, D_MODEL), new_state)
```

```python
import functools

import numpy as np
import jax
import jax.numpy as jnp
from jax import lax
from jax.experimental import pallas as pl
from jax.experimental.pallas import tpu as pltpu

D_MODEL = 1024
DEPTH = 2
GRID_W = 64
FOURIER_W = 512
FOURIER_HEADS = 4
FOURIER_HD = FOURIER_W // FOURIER_HEADS
GLA_DV_W = 512
GLA_DK_W = 256
GLA_HEADS = 4
DV = GLA_DV_W // GLA_HEADS
DK = GLA_DK_W // GLA_HEADS
GATE_RANK = 16
GATE_TEMP = 16.0
CHUNK = 64
D_FF = 11 * D_MODEL // 4
EPS = 1e-6
POS_BASE = 10000.0
S_F = FOURIER_W
S_Q = S_F + GLA_DK_W
S_K = S_Q + GLA_DK_W
S_V = S_K + GLA_DV_W
S_AF = S_V + GATE_RANK
S_AB = S_AF + GATE_RANK
IN_COLS = S_AB + GLA_DV_W

C_F = 0
C_Q = C_F + FOURIER_W
C_K = C_Q + GLA_DK_W
C_V = C_K + GLA_DK_W
C_G = C_V + GLA_DV_W
C_A = C_G + GLA_DV_W
LANE = 128
IN_COLS_PAD = C_A + LANE
MOD_ROWS = 8
BLK = 256

VMEM_LIMIT = 56 * 1024 * 1024

F32 = jnp.float32
BF16 = jnp.bfloat16


def _silu(x):
    return x * jax.nn.sigmoid(x)


def _rms(x, g):
    return x * lax.rsqrt(jnp.mean(x * x, axis=-1, keepdims=True) + EPS) * g


def _dot(a, b):
    return jnp.dot(a, b, preferred_element_type=F32)


def _dot_nt(a, b):
    return lax.dot_general(a, b, (((1,), (1,)), ((), ())), preferred_element_type=F32)


def _ada_kernel(cv_ref, w_ref, b_ref, o_ref):
    s = _silu(cv_ref[...]).astype(BF16)
    o_ref[0] = _dot(s, w_ref[0].astype(BF16)) + b_ref[0]


def _ada(cv, w_ada, b_ada):
    tn = 1536
    n_out = w_ada.shape[-1]
    return pl.pallas_call(
        _ada_kernel,
        grid=(DEPTH, n_out // tn),
        in_specs=[
            pl.BlockSpec((MOD_ROWS, D_MODEL), lambda l, n: (0, 0)),
            pl.BlockSpec((1, D_MODEL, tn), lambda l, n: (l, 0, n)),
            pl.BlockSpec((1, 1, tn), lambda l, n: (l, 0, n)),
        ],
        out_specs=pl.BlockSpec((1, MOD_ROWS, tn), lambda l, n: (l, 0, n)),
        out_shape=jax.ShapeDtypeStruct((DEPTH, MOD_ROWS, n_out), F32),
        compiler_params=pltpu.CompilerParams(
            dimension_semantics=("arbitrary", "arbitrary"), vmem_limit_bytes=VMEM_LIMIT),
        name="ada",
    )(cv, w_ada, b_ada.reshape(DEPTH, 1, n_out))


def _inproj_kernel(*refs, has_pe):
    if has_pe:
        x_ref, pe_ref, *refs = refs
    else:
        x_ref, *refs = refs
    (mod_ref, g_ref, w_ref, cc_ref, wg_ref, bg_ref,
     ab_ref, q_ref, k_ref, v_ref, laf_ref, lab_ref, sg_ref) = refs
    x = x_ref[...]
    if has_pe:
        x = x + pe_ref[...]
    mod = mod_ref[0]
    shift = mod[:, 0:D_MODEL]
    scale = mod[:, D_MODEL:2 * D_MODEL]
    h = (_rms(x, g_ref[...]) * (1.0 + scale) + shift).astype(BF16)
    z = _dot(h, w_ref[...])
    zf = z[:, C_F:C_Q].astype(BF16)
    for hh in range(FOURIER_HEADS):
        sl = slice(hh * FOURIER_HD, (hh + 1) * FOURIER_HD)
        cs = _dot(zf[:, sl], cc_ref[...])
        ab_ref[0, :, sl] = cs[:, :FOURIER_HD].astype(BF16)
        ab_ref[1, :, sl] = cs[:, FOURIER_HD:].astype(BF16)
    q_ref[...] = z[:, C_Q:C_K] * (DK ** -0.5)
    k_ref[...] = z[:, C_K:C_V]
    v_ref[...] = z[:, C_V:C_G].astype(BF16)
    sg_ref[...] = _silu(z[:, C_G:C_A]).astype(BF16)
    xg = _dot(z[:, C_A:IN_COLS_PAD].astype(BF16), wg_ref[...]) + bg_ref[...]
    la = (jnp.minimum(xg, 0.0) - jnp.log1p(jnp.exp(-jnp.abs(xg)))) * (1.0 / GATE_TEMP)
    laf_ref[...] = la[:, :GLA_DK_W]
    lab_ref[...] = la[:, GLA_DK_W:]


def _inproj(x, pe, mod, g, w_in_r, cc, wg, bg, seq_len, tm):
    n = x.shape[0]
    has_pe = pe is not None
    per_batch = mod.shape[0] > 1
    mod_idx = (lambda i: ((i * tm) // seq_len, 0, 0)) if per_batch else (lambda i: (0, 0, 0))
    row = lambda w: pl.BlockSpec((tm, w), lambda i: (i, 0))
    full = lambda a: pl.BlockSpec(a.shape, lambda i: (0,) * a.ndim)
    in_specs = [row(D_MODEL)]
    args = [x]
    if has_pe:
        in_specs.append(pl.BlockSpec((tm, D_MODEL), lambda i: (i % (seq_len // tm), 0)))
        args.append(pe)
    in_specs += [pl.BlockSpec((1, 1, 6 * D_MODEL), mod_idx), full(g), full(w_in_r), full(cc),
                 full(wg), full(bg)]
    args += [mod, g, w_in_r, cc, wg, bg]
    out_shape = (
        jax.ShapeDtypeStruct((2, n, FOURIER_W), BF16),
        jax.ShapeDtypeStruct((n, GLA_DK_W), F32),
        jax.ShapeDtypeStruct((n, GLA_DK_W), F32),
        jax.ShapeDtypeStruct((n, GLA_DV_W), BF16),
        jax.ShapeDtypeStruct((n, GLA_DK_W), F32),
        jax.ShapeDtypeStruct((n, GLA_DK_W), F32),
        jax.ShapeDtypeStruct((n, GLA_DV_W), BF16),
    )
    out_specs = (
        pl.BlockSpec((2, tm, FOURIER_W), lambda i: (0, i, 0)),
        row(GLA_DK_W), row(GLA_DK_W), row(GLA_DV_W), row(GLA_DK_W), row(GLA_DK_W), row(GLA_DV_W),
    )
    return pl.pallas_call(
        functools.partial(_inproj_kernel, has_pe=has_pe),
        grid=(n // tm,),
        in_specs=in_specs,
        out_specs=out_specs,
        out_shape=out_shape,
        compiler_params=pltpu.CompilerParams(
            dimension_semantics=("arbitrary",), vmem_limit_bytes=VMEM_LIMIT),
        name="inproj",
    )(*args)


def _seqdft_kernel(c_ref, s_ref, a_ref, b_ref, o_ref):
    o_ref[...] = (_dot(c_ref[...], a_ref[...]) - _dot(s_ref[...], b_ref[...])).astype(BF16)


def _seqdft(ab, ct, st, batch, seq_len, tr):
    ab4 = ab.reshape(2, batch, seq_len, FOURIER_W)
    return pl.pallas_call(
        _seqdft_kernel,
        grid=(batch, seq_len // tr),
        in_specs=[
            pl.BlockSpec((tr, seq_len), lambda b, r: (r, 0)),
            pl.BlockSpec((tr, seq_len), lambda b, r: (r, 0)),
            pl.BlockSpec((None, None, seq_len, FOURIER_W), lambda b, r: (0, b, 0, 0)),
            pl.BlockSpec((None, None, seq_len, FOURIER_W), lambda b, r: (1, b, 0, 0)),
        ],
        out_specs=pl.BlockSpec((None, tr, FOURIER_W), lambda b, r: (b, r, 0)),
        out_shape=jax.ShapeDtypeStruct((batch, seq_len, FOURIER_W), BF16),
        compiler_params=pltpu.CompilerParams(
            dimension_semantics=("arbitrary", "arbitrary"), vmem_limit_bytes=VMEM_LIMIT),
        name="seqdft",
    )(ct, st, ab4, ab4)


def _gla_kernel(*refs, seq_len, has_s0):
    if has_s0:
        s0_ref, *refs = refs
    (q_ref, k_ref, v_ref, laf_ref, lab_ref, sg_ref, g_ref,
     og_ref, sT_ref,
     qt_s, kt_s, ke_s, dec_s, o_s, st_s) = refs
    nc = seq_len // CHUNK
    la_refs = (laf_ref, lab_ref)

    ri = lax.broadcasted_iota(jnp.int32, (BLK, BLK), 0)
    ci = lax.broadcasted_iota(jnp.int32, (BLK, BLK), 1)
    same = (ri // CHUNK) == (ci // CHUNK)
    tri = (jnp.where(same & (ci <= ri), 1.0, 0.0).astype(BF16),
           jnp.where(same & (ci >= ri), 1.0, 0.0).astype(BF16))
    tot_m = jnp.where(same, 1.0, 0.0).astype(BF16)

    for blk in range(seq_len // BLK):
        rows = slice(blk * BLK, (blk + 1) * BLK)
        q = q_ref[rows, :]
        k = k_ref[rows, :]
        for d in range(2):
            la = la_refs[d][rows, :]
            hi = la.astype(BF16)
            lo = (la - hi.astype(F32)).astype(BF16)
            b = _dot(tri[d], hi) + _dot(tri[d], lo)
            tot = _dot(tot_m, hi) + _dot(tot_m, lo)
            qt_s[d, rows, :] = (q * jnp.exp(b)).astype(BF16)
            kt_s[d, rows, :] = (k * jnp.exp(-b)).astype(BF16)
            ke_s[d, rows, :] = (k * jnp.exp(tot - b)).astype(BF16)
            dec_s[d, rows, :] = jnp.exp(tot)
    o_s[...] = jnp.zeros_like(o_s)
    for d in range(2):
        if has_s0:
            st_s[d] = s0_ref[d]
        else:
            st_s[d] = jnp.zeros((DV, GLA_DK_W), F32)

    lane_head = lax.broadcasted_iota(jnp.int32, (1, GLA_DK_W), 1) // DK
    head_mask = [jnp.where(lane_head == h, 1.0, 0.0).astype(BF16) for h in range(GLA_HEADS)]
    ti = lax.broadcasted_iota(jnp.int32, (CHUNK, CHUNK), 0)
    tj = lax.broadcasted_iota(jnp.int32, (CHUNK, CHUNK), 1)
    causal = (tj <= ti, tj >= ti)

    def chunk_step(d, c):
        r0 = c * CHUNK
        if not isinstance(r0, int):
            r0 = pl.multiple_of(r0, CHUNK)
        rows = pl.ds(r0, CHUNK)
        qt_c = qt_s[d, rows, :]
        q_stack = jnp.concatenate([qt_c * head_mask[h] for h in range(GLA_HEADS)], axis=0)
        w = jnp.concatenate([st_s[d].astype(BF16), kt_s[d, rows, :]], axis=0)
        res = _dot_nt(q_stack, w)
        v_c = v_ref[rows, :]
        for h in range(GLA_HEADS):
            hr = slice(h * CHUNK, (h + 1) * CHUNK)
            hv = slice(h * DV, (h + 1) * DV)
            att = jnp.where(causal[d], res[hr, DV:DV + CHUNK], 0.0).astype(BF16)
            o_s[rows, hv] += res[hr, 0:DV] + _dot(att, v_c[:, hv])
        ke_c = ke_s[d, rows, :]
        k_bd = jnp.concatenate([ke_c * head_mask[h] for h in range(GLA_HEADS)], axis=0)
        v_stack = jnp.concatenate(
            [v_c[:, h * DV:(h + 1) * DV] for h in range(GLA_HEADS)], axis=0)
        v_stack_t = v_stack.astype(F32).T.astype(BF16)
        st_s[d] = dec_s[d, pl.ds(r0, 1), :] * st_s[d] + _dot(v_stack_t, k_bd)

    if nc <= 4:
        for i in range(nc):
            chunk_step(0, i)
            chunk_step(1, nc - 1 - i)
    else:
        def body(i, carry):
            chunk_step(0, i)
            chunk_step(1, nc - 1 - i)
            return carry
        lax.fori_loop(0, nc, body, 0)

    g = g_ref[...]
    for h in range(GLA_HEADS):
        hv = slice(h * DV, (h + 1) * DV)
        og_ref[:, hv] = (_rms(o_s[:, hv], g) * sg_ref[:, hv].astype(F32)).astype(BF16)
    sT_ref[...] = st_s[...]


def _gla(q, k, v, laf, lab, sg, g_gla, s0t, batch, seq_len):
    has_s0 = s0t is not None
    r3 = lambda a: a.reshape(batch, seq_len, a.shape[-1])
    seq = lambda w: pl.BlockSpec((None, seq_len, w), lambda b: (b, 0, 0))
    st_spec = pl.BlockSpec((None, 2, DV, GLA_DK_W), lambda b: (b, 0, 0, 0))
    in_specs = []
    args = []
    if has_s0:
        in_specs.append(st_spec)
        args.append(s0t)
    in_specs += [seq(GLA_DK_W), seq(GLA_DK_W), seq(GLA_DV_W), seq(GLA_DK_W), seq(GLA_DK_W),
                 seq(GLA_DV_W), pl.BlockSpec((1, DV), lambda b: (0, 0))]
    args += [r3(q), r3(k), r3(v), r3(laf), r3(lab), r3(sg), g_gla]
    return pl.pallas_call(
        functools.partial(_gla_kernel, seq_len=seq_len, has_s0=has_s0),
        grid=(batch,),
        in_specs=in_specs,
        out_specs=(seq(GLA_DV_W), st_spec),
        out_shape=(jax.ShapeDtypeStruct((batch, seq_len, GLA_DV_W), BF16),
                   jax.ShapeDtypeStruct((batch, 2, DV, GLA_DK_W), F32)),
        scratch_shapes=[
            pltpu.VMEM((2, seq_len, GLA_DK_W), BF16),
            pltpu.VMEM((2, seq_len, GLA_DK_W), BF16),
            pltpu.VMEM((2, seq_len, GLA_DK_W), BF16),
            pltpu.VMEM((2, seq_len, GLA_DK_W), F32),
            pltpu.VMEM((seq_len, GLA_DV_W), F32),
            pltpu.VMEM((2, DV, GLA_DK_W), F32),
        ],
        compiler_params=pltpu.CompilerParams(
            dimension_semantics=("arbitrary",), vmem_limit_bytes=VMEM_LIMIT),
        name="gla",
    )(*args)


FF_CHUNK = 256


def _tail_kernel(*refs, has_pe, seg, tm):
    if has_pe:
        x_ref, pe_ref, *refs = refs
    else:
        x_ref, *refs = refs
    (yf_ref, og_ref, mod_ref, wo_ref, gpm_ref, gpf_ref, gqf_ref,
     wup_ref, cw_ref, cb_ref, wdn_ref, out_ref, x1_s, a_s) = refs
    mod = mod_ref[0]
    x = x_ref[...]
    if has_pe:
        x = x + pe_ref[...]
    y = _dot(yf_ref[...], wo_ref[0:FOURIER_W, :]) + _dot(og_ref[...], wo_ref[FOURIER_W:, :])
    gate_m = mod[:, 2 * D_MODEL:3 * D_MODEL]
    x1 = x + gate_m * _rms(y, gpm_ref[...])
    x1_s[...] = x1
    shift_f = mod[:, 3 * D_MODEL:4 * D_MODEL]
    scale_f = mod[:, 4 * D_MODEL:5 * D_MODEL]
    h = (_rms(x1, gpf_ref[...]) * (1.0 + scale_f) + shift_f).astype(BF16)

    pos = lax.broadcasted_iota(jnp.int32, (tm, 1), 0) % seg
    first = pos == 0
    last = pos == seg - 1

    def conv_cols(c0):
        cols = slice(c0, c0 + FF_CHUNK)
        u = _dot(h, wup_ref[:, cols])
        u_prev = jnp.where(first, 0.0, pltpu.roll(u, 1, 0))
        u_next = jnp.where(last, 0.0, pltpu.roll(u, tm - 1, 0))
        return (u_prev * cw_ref[0:1, cols] + u * cw_ref[1:2, cols] + u_next * cw_ref[2:3, cols]
                + cb_ref[:, cols])

    for c in range(D_FF // FF_CHUNK):
        val = conv_cols(c * FF_CHUNK)
        gate = conv_cols(D_FF + c * FF_CHUNK)
        a_s[:, c * FF_CHUNK:(c + 1) * FF_CHUNK] = (_silu(gate) * val).astype(BF16)

    y2 = _dot(a_s[...], wdn_ref[...])
    gate_f = mod[:, 5 * D_MODEL:6 * D_MODEL]
    out_ref[...] = x1_s[...] + gate_f * _rms(y2, gqf_ref[...])


def _tail(x, pe, yf, og, mod, w_out, gpm, gpf, gqf, w_up, cw, cb, w_down, seq_len, seg, tm):
    n = x.shape[0]
    has_pe = pe is not None
    per_batch = mod.shape[0] > 1
    mod_idx = (lambda i: ((i * tm) // seq_len, 0, 0)) if per_batch else (lambda i: (0, 0, 0))
    row = lambda w: pl.BlockSpec((tm, w), lambda i: (i, 0))
    full = lambda a: pl.BlockSpec(a.shape, lambda i: (0,) * a.ndim, pipeline_mode=pl.Buffered(1))
    in_specs = [row(D_MODEL)]
    args = [x]
    if has_pe:
        in_specs.append(pl.BlockSpec((tm, D_MODEL), lambda i: (i % (seq_len // tm), 0)))
        args.append(pe)
    in_specs += [
        row(FOURIER_W), row(GLA_DV_W), pl.BlockSpec((1, 1, 6 * D_MODEL), mod_idx),
        full(w_out), full(gpm), full(gpf), full(gqf), full(w_up), full(cw), full(cb), full(w_down),
    ]
    args += [yf, og, mod, w_out, gpm, gpf, gqf, w_up, cw, cb, w_down]
    return pl.pallas_call(
        functools.partial(_tail_kernel, has_pe=has_pe, seg=seg, tm=tm),
        grid=(n // tm,),
        in_specs=in_specs,
        out_specs=row(D_MODEL),
        out_shape=jax.ShapeDtypeStruct((n, D_MODEL), F32),
        scratch_shapes=[
            pltpu.VMEM((tm, D_MODEL), F32),
            pltpu.VMEM((tm, D_FF), BF16),
        ],
        compiler_params=pltpu.CompilerParams(
            dimension_semantics=("arbitrary",), vmem_limit_bytes=VMEM_LIMIT),
        name="tail",
    )(*args)


def _grid_pos_embed(rows, d):
    quarter = d // 4
    omega = 1.0 / (POS_BASE ** (jnp.arange(quarter, dtype=F32) / quarter))
    er = jnp.arange(rows, dtype=F32)[:, None] * omega
    ec = jnp.arange(GRID_W, dtype=F32)[:, None] * omega
    pr = jnp.concatenate([jnp.sin(er), jnp.cos(er)], axis=-1)
    pc = jnp.concatenate([jnp.sin(ec), jnp.cos(ec)], axis=-1)
    pe = jnp.concatenate([jnp.broadcast_to(pr[:, None], (rows, GRID_W, d // 2)),
                          jnp.broadcast_to(pc[None], (rows, GRID_W, d // 2))], axis=-1)
    return pe.reshape(rows * GRID_W, d)


def _dft_tables(n, scale):
    idx = jnp.arange(n, dtype=jnp.int32)
    ang = ((idx[:, None] * idx[None, :]) % n).astype(F32) * np.float32(2.0 * np.pi / n)
    return jnp.cos(ang) * np.float32(scale), jnp.sin(ang) * np.float32(scale)


def _layer_weights(l, w_in, w_gate_f, b_gate_f, w_gate_b, b_gate_b, w_out, w_up, conv_w, conv_b,
                   w_down):
    wi = w_in[l]
    w_in_r = jnp.concatenate(
        [wi[:, :S_V], wi[:, S_AB:], wi[:, S_V:S_AB],
         jnp.zeros((D_MODEL, LANE - 2 * GATE_RANK), F32)], axis=1).astype(BF16)
    wg = jnp.zeros((LANE, 2 * GLA_DK_W), F32)
    wg = wg.at[0:GATE_RANK, 0:GLA_DK_W].set(w_gate_f[l])
    wg = wg.at[GATE_RANK:2 * GATE_RANK, GLA_DK_W:].set(w_gate_b[l])
    bg = jnp.concatenate([b_gate_f[l], b_gate_b[l]])[None]

    return dict(
        w_in_r=w_in_r, wg=wg.astype(BF16), bg=bg, w_out=w_out[l].astype(BF16),
        w_up=w_up[l].astype(BF16), cw=conv_w[l], cb=conv_b[l][None],
        w_down=w_down[l].astype(BF16))


def _stream_layer(x, pe, mod, s0t, lw, gains, dft, batch, seq_len, seg, tm_in, tr, tm_tail):
    g_pre_mix, g_post_mix, g_pre_ffn, g_post_ffn, g_gla = gains
    cc, ct, st = dft
    ab, q, k, v, laf, lab, sg = _inproj(x, pe, mod, g_pre_mix, lw['w_in_r'], cc, lw['wg'],
                                        lw['bg'], seq_len, tm_in)
    yf = _seqdft(ab, ct, st, batch, seq_len, tr).reshape(batch * seq_len, FOURIER_W)
    og, s_t = _gla(q, k, v, laf, lab, sg, g_gla, s0t, batch, seq_len)
    x_new = _tail(x, pe, yf, og.reshape(batch * seq_len, GLA_DV_W), mod, lw['w_out'], g_post_mix,
                  g_pre_ffn, g_post_ffn, lw['w_up'], lw['cw'], lw['cb'], lw['w_down'],
                  seq_len, seg, tm_tail)
    return x_new, s_t


def kernel(x_prompt, x_sample, state_gla, c, c_ctx, g_pre_mix, g_post_mix, g_pre_ffn, g_post_ffn,
           w_ada, b_ada, w_in, w_gate_f, b_gate_f, w_gate_b, b_gate_b, g_gla, w_out, w_up, conv_w,
           conv_b, w_down):
    pb, pt, _ = x_prompt.shape
    sb, s_t, _ = x_sample.shape
    rows = s_t // GRID_W

    cv = jnp.concatenate([c_ctx[None], c, jnp.zeros((MOD_ROWS - 1 - sb, D_MODEL), F32)], axis=0)
    mod_all = _ada(cv, w_ada, b_ada)

    cc_c, cc_s = _dft_tables(FOURIER_HD, FOURIER_HD ** -0.5)
    cc = jnp.concatenate([cc_c, cc_s], axis=1).astype(BF16)
    dft_p = (cc,) + tuple(t.astype(BF16) for t in _dft_tables(pt, pt ** -0.5))
    dft_s = (cc,) + tuple(t.astype(BF16) for t in _dft_tables(s_t, s_t ** -0.5))
    pe = _grid_pos_embed(rows, D_MODEL)

    xp = x_prompt.reshape(pb * pt, D_MODEL)
    xs = x_sample.reshape(sb * s_t, D_MODEL)
    ctx_states = []
    for l in range(DEPTH):
        lw = _layer_weights(l, w_in, w_gate_f, b_gate_f, w_gate_b, b_gate_b, w_out, w_up, conv_w,
                            conv_b, w_down)
        gains = (g_pre_mix[l][None], g_post_mix[l][None], g_pre_ffn[l][None],
                 g_post_ffn[l][None], g_gla[l][None])
        mod_p = mod_all[l, 0:1][:, None]
        mod_s = mod_all[l, 1:1 + sb][:, None]
        xp, st_p = _stream_layer(xp, None, mod_p, None, lw, gains, dft_p, pb, pt, pt,
                                 512, pt, 512)
        ctx_states.append(
            st_p.reshape(pb, 2, DV, GLA_HEADS, DK).transpose(0, 1, 3, 4, 2))
        s0t = state_gla[:, l].transpose(0, 1, 4, 2, 3).reshape(sb, 2, DV, GLA_DK_W)
        xs, _ = _stream_layer(xs, pe if l == 0 else None, mod_s, s0t, lw, gains, dft_s, sb, s_t,
                              GRID_W, 512, 512, 512)
    new_state = jnp.stack(ctx_states, axis=1)
    return (xp.reshape(pb, pt, D_MODEL), xs.reshape(sb, s_t, D_MODEL), new_state)
```

```python
import functools

import numpy as np
import jax
import jax.numpy as jnp
from jax import lax
from jax.experimental import pallas as pl
from jax.experimental.pallas import tpu as pltpu

D_MODEL = 1024
DEPTH = 2
GRID_W = 64
FOURIER_W = 512
FOURIER_HEADS = 4
FOURIER_HD = FOURIER_W // FOURIER_HEADS
GLA_DV_W = 512
GLA_DK_W = 256
GLA_HEADS = 4
DV = GLA_DV_W // GLA_HEADS
DK = GLA_DK_W // GLA_HEADS
GATE_RANK = 16
GATE_TEMP = 16.0
CHUNK = 64
D_FF = 11 * D_MODEL // 4
EPS = 1e-6
POS_BASE = 10000.0
S_F = FOURIER_W
S_Q = S_F + GLA_DK_W
S_K = S_Q + GLA_DK_W
S_V = S_K + GLA_DV_W
S_AF = S_V + GATE_RANK
S_AB = S_AF + GATE_RANK
IN_COLS = S_AB + GLA_DV_W

C_F = 0
C_Q = C_F + FOURIER_W
C_K = C_Q + GLA_DK_W
C_V = C_K + GLA_DK_W
C_G = C_V + GLA_DV_W
C_A = C_G + GLA_DV_W
LANE = 128
IN_COLS_PAD = C_A + LANE
MOD_ROWS = 8
BLK = 256

VMEM_LIMIT = 56 * 1024 * 1024

F32 = jnp.float32
BF16 = jnp.bfloat16


def _silu(x):
    return x * jax.nn.sigmoid(x)


def _rms(x, g):
    return x * lax.rsqrt(jnp.mean(x * x, axis=-1, keepdims=True) + EPS) * g


def _dot(a, b):
    return jnp.dot(a, b, preferred_element_type=F32)


def _dot_nt(a, b):
    return lax.dot_general(a, b, (((1,), (1,)), ((), ())), preferred_element_type=F32)


def _ada_kernel(cv_ref, w_ref, b_ref, o_ref):
    s = _silu(cv_ref[...]).astype(BF16)
    o_ref[0] = _dot(s, w_ref[0].astype(BF16)) + b_ref[0]


def _ada(cv, w_ada, b_ada):
    tn = 1536
    n_out = w_ada.shape[-1]
    return pl.pallas_call(
        _ada_kernel,
        grid=(DEPTH, n_out // tn),
        in_specs=[
            pl.BlockSpec((MOD_ROWS, D_MODEL), lambda l, n: (0, 0)),
            pl.BlockSpec((1, D_MODEL, tn), lambda l, n: (l, 0, n)),
            pl.BlockSpec((1, 1, tn), lambda l, n: (l, 0, n)),
        ],
        out_specs=pl.BlockSpec((1, MOD_ROWS, tn), lambda l, n: (l, 0, n)),
        out_shape=jax.ShapeDtypeStruct((DEPTH, MOD_ROWS, n_out), F32),
        compiler_params=pltpu.CompilerParams(
            dimension_semantics=("arbitrary", "arbitrary"), vmem_limit_bytes=VMEM_LIMIT),
        name="ada",
    )(cv, w_ada, b_ada.reshape(DEPTH, 1, n_out))


def _inproj_kernel(*refs, has_pe):
    if has_pe:
        x_ref, pe_ref, *refs = refs
    else:
        x_ref, *refs = refs
    (mod_ref, g_ref, w_ref, cc_ref, wg_ref, bg_ref,
     ab_ref, q_ref, k_ref, v_ref, laf_ref, lab_ref, sg_ref) = refs
    x = x_ref[...]
    if has_pe:
        x = x + pe_ref[...]
    mod = mod_ref[0]
    shift = mod[:, 0:D_MODEL]
    scale = mod[:, D_MODEL:2 * D_MODEL]
    h = (_rms(x, g_ref[...]) * (1.0 + scale) + shift).astype(BF16)
    z = _dot(h, w_ref[...])
    zf = z[:, C_F:C_Q].astype(BF16)
    for hh in range(FOURIER_HEADS):
        sl = slice(hh * FOURIER_HD, (hh + 1) * FOURIER_HD)
        cs = _dot(zf[:, sl], cc_ref[...])
        ab_ref[0, :, sl] = cs[:, :FOURIER_HD].astype(BF16)
        ab_ref[1, :, sl] = cs[:, FOURIER_HD:].astype(BF16)
    q_ref[...] = z[:, C_Q:C_K] * (DK ** -0.5)
    k_ref[...] = z[:, C_K:C_V]
    v_ref[...] = z[:, C_V:C_G].astype(BF16)
    sg_ref[...] = _silu(z[:, C_G:C_A]).astype(BF16)
    xg = _dot(z[:, C_A:IN_COLS_PAD].astype(BF16), wg_ref[...]) + bg_ref[...]
    la = (jnp.minimum(xg, 0.0) - jnp.log1p(jnp.exp(-jnp.abs(xg)))) * (1.0 / GATE_TEMP)
    laf_ref[...] = la[:, :GLA_DK_W]
    lab_ref[...] = la[:, GLA_DK_W:]


def _inproj(x, pe, mod, g, w_in_r, cc, wg, bg, seq_len, tm):
    n = x.shape[0]
    has_pe = pe is not None
    per_batch = mod.shape[0] > 1
    mod_idx = (lambda i: ((i * tm) // seq_len, 0, 0)) if per_batch else (lambda i: (0, 0, 0))
    row = lambda w: pl.BlockSpec((tm, w), lambda i: (i, 0))
    full = lambda a: pl.BlockSpec(a.shape, lambda i: (0,) * a.ndim)
    in_specs = [row(D_MODEL)]
    args = [x]
    if has_pe:
        in_specs.append(pl.BlockSpec((tm, D_MODEL), lambda i: (i % (seq_len // tm), 0)))
        args.append(pe)
    in_specs += [pl.BlockSpec((1, 1, 6 * D_MODEL), mod_idx), full(g), full(w_in_r), full(cc),
                 full(wg), full(bg)]
    args += [mod, g, w_in_r, cc, wg, bg]
    out_shape = (
        jax.ShapeDtypeStruct((2, n, FOURIER_W), BF16),
        jax.ShapeDtypeStruct((n, GLA_DK_W), F32),
        jax.ShapeDtypeStruct((n, GLA_DK_W), F32),
        jax.ShapeDtypeStruct((n, GLA_DV_W), BF16),
        jax.ShapeDtypeStruct((n, GLA_DK_W), F32),
        jax.ShapeDtypeStruct((n, GLA_DK_W), F32),
        jax.ShapeDtypeStruct((n, GLA_DV_W), BF16),
    )
    out_specs = (
        pl.BlockSpec((2, tm, FOURIER_W), lambda i: (0, i, 0)),
        row(GLA_DK_W), row(GLA_DK_W), row(GLA_DV_W), row(GLA_DK_W), row(GLA_DK_W), row(GLA_DV_W),
    )
    return pl.pallas_call(
        functools.partial(_inproj_kernel, has_pe=has_pe),
        grid=(n // tm,),
        in_specs=in_specs,
        out_specs=out_specs,
        out_shape=out_shape,
        compiler_params=pltpu.CompilerParams(
            dimension_semantics=("arbitrary",), vmem_limit_bytes=VMEM_LIMIT),
        name="inproj",
    )(*args)


DFT_P = 256


def _seqdft_kernel(*refs, radix, nb):
    if radix > 1:
        c_ref, sn_ref, twc_ref, tws_ref, a_ref, b_ref, o_ref = refs
    else:
        c_ref, sn_ref, a_ref, b_ref, o_ref = refs
    c = c_ref[...]
    sn = sn_ref[...]
    for bi in range(nb):
        a = a_ref[bi]
        b = b_ref[bi]
        g_re = _dot(c, a) + _dot(sn, b)
        if radix == 1:
            o_ref[bi, 0] = g_re.astype(BF16)
            continue
        g_im = _dot(sn, a) - _dot(c, b)
        for k2 in range(radix):
            acc = g_re[:, 0:FOURIER_W]
            for r in range(1, radix):
                j = k2 * radix + r
                cols = slice(r * FOURIER_W, (r + 1) * FOURIER_W)
                acc = acc + twc_ref[:, j:j + 1] * g_re[:, cols] + tws_ref[:, j:j + 1] * g_im[:, cols]
            o_ref[bi, k2] = acc.astype(BF16)


def _seqdft(ab, tables, batch, seq_len, nb):
    radix = seq_len // DFT_P
    width = radix * FOURIER_W
    ab4 = ab.reshape(2, batch, DFT_P, width)
    full = lambda a: pl.BlockSpec(a.shape, lambda i: (0,) * a.ndim)
    out = pl.pallas_call(
        functools.partial(_seqdft_kernel, radix=radix, nb=nb),
        grid=(batch // nb,),
        in_specs=[full(t) for t in tables] + [
            pl.BlockSpec((None, nb, DFT_P, width), lambda i: (0, i, 0, 0)),
            pl.BlockSpec((None, nb, DFT_P, width), lambda i: (1, i, 0, 0)),
        ],
        out_specs=pl.BlockSpec((nb, radix, DFT_P, FOURIER_W), lambda i: (i, 0, 0, 0)),
        out_shape=jax.ShapeDtypeStruct((batch, radix, DFT_P, FOURIER_W), BF16),
        compiler_params=pltpu.CompilerParams(
            dimension_semantics=("arbitrary",), vmem_limit_bytes=VMEM_LIMIT),
        name="seqdft",
    )(*tables, ab4, ab4)
    return out.reshape(batch * seq_len, FOURIER_W)


def _gla_kernel(*refs, seq_len, has_s0):
    if has_s0:
        s0_ref, *refs = refs
    (q_ref, k_ref, v_ref, laf_ref, lab_ref, sg_ref, g_ref,
     og_ref, sT_ref,
     qt_s, kt_s, ke_s, dec_s, o_s, st_s) = refs
    nc = seq_len // CHUNK
    la_refs = (laf_ref, lab_ref)

    ri = lax.broadcasted_iota(jnp.int32, (BLK, BLK), 0)
    ci = lax.broadcasted_iota(jnp.int32, (BLK, BLK), 1)
    same = (ri // CHUNK) == (ci // CHUNK)
    tri = (jnp.where(same & (ci <= ri), 1.0, 0.0).astype(BF16),
           jnp.where(same & (ci >= ri), 1.0, 0.0).astype(BF16))
    tot_m = jnp.where(same, 1.0, 0.0).astype(BF16)

    for blk in range(seq_len // BLK):
        rows = slice(blk * BLK, (blk + 1) * BLK)
        q = q_ref[rows, :]
        k = k_ref[rows, :]
        for d in range(2):
            la = la_refs[d][rows, :]
            hi = la.astype(BF16)
            lo = (la - hi.astype(F32)).astype(BF16)
            b = _dot(tri[d], hi) + _dot(tri[d], lo)
            tot = _dot(tot_m, hi) + _dot(tot_m, lo)
            qt_s[d, rows, :] = (q * jnp.exp(b)).astype(BF16)
            kt_s[d, rows, :] = (k * jnp.exp(-b)).astype(BF16)
            ke_s[d, rows, :] = (k * jnp.exp(tot - b)).astype(BF16)
            for cc in range(BLK // CHUNK):
                c = blk * (BLK // CHUNK) + cc
                dec_s[d, c:c + 1, :] = jnp.exp(tot[cc * CHUNK:cc * CHUNK + 1, :])
    for d in range(2):
        if has_s0:
            st_s[d] = s0_ref[d]
        else:
            st_s[d] = jnp.zeros((DV, GLA_DK_W), F32)

    lane_head = lax.broadcasted_iota(jnp.int32, (1, GLA_DK_W), 1) // DK
    head_mask = [jnp.where(lane_head == h, 1.0, 0.0).astype(BF16) for h in range(GLA_HEADS)]
    ti = lax.broadcasted_iota(jnp.int32, (CHUNK, CHUNK), 0)
    tj = lax.broadcasted_iota(jnp.int32, (CHUNK, CHUNK), 1)
    causal = (tj <= ti, tj >= ti)

    def pair_step(chunks):
        rows = []
        for c in chunks:
            r0 = c * CHUNK
            rows.append(pl.ds(r0 if isinstance(r0, int) else pl.multiple_of(r0, CHUNK), CHUNK))
        v_c = [v_ref[rows[d], :] for d in range(2)]
        res = []
        for d in range(2):
            qt_c = qt_s[d, rows[d], :]
            q_stack = jnp.concatenate([qt_c * head_mask[h] for h in range(GLA_HEADS)], axis=0)
            w = jnp.concatenate([st_s[d].astype(BF16), kt_s[d, rows[d], :]], axis=0)
            res.append(_dot_nt(q_stack, w))
        kv = []
        for d in range(2):
            ke_c = ke_s[d, rows[d], :]
            k_bd = jnp.concatenate([ke_c * head_mask[h] for h in range(GLA_HEADS)], axis=0)
            v_stack = jnp.concatenate(
                [v_c[d][:, h * DV:(h + 1) * DV] for h in range(GLA_HEADS)], axis=0)
            v_stack_t = v_stack.astype(F32).T.astype(BF16)
            kv.append(_dot(v_stack_t, k_bd))
        for d in range(2):
            c = chunks[d]
            st_s[d] = dec_s[d, pl.ds(c, 1), :] * st_s[d] + kv[d]
        for d in range(2):
            for h in range(GLA_HEADS):
                hr = slice(h * CHUNK, (h + 1) * CHUNK)
                hv = slice(h * DV, (h + 1) * DV)
                att = jnp.where(causal[d], res[d][hr, DV:DV + CHUNK], 0.0).astype(BF16)
                o_s[d, rows[d], hv] = res[d][hr, 0:DV] + _dot(att, v_c[d][:, hv])

    if nc <= 4:
        for i in range(nc):
            pair_step((i, nc - 1 - i))
    else:
        def body(i, carry):
            pair_step((i, nc - 1 - i))
            return carry
        lax.fori_loop(0, nc, body, 0, unroll=4)

    g = g_ref[...]
    for h in range(GLA_HEADS):
        hv = slice(h * DV, (h + 1) * DV)
        o = o_s[0, :, hv] + o_s[1, :, hv]
        og_ref[:, hv] = (_rms(o, g) * sg_ref[:, hv].astype(F32)).astype(BF16)
    sT_ref[...] = st_s[...]


def _gla(q, k, v, laf, lab, sg, g_gla, s0t, batch, seq_len):
    has_s0 = s0t is not None
    r3 = lambda a: a.reshape(batch, seq_len, a.shape[-1])
    seq = lambda w: pl.BlockSpec((None, seq_len, w), lambda b: (b, 0, 0))
    st_spec = pl.BlockSpec((None, 2, DV, GLA_DK_W), lambda b: (b, 0, 0, 0))
    in_specs = []
    args = []
    if has_s0:
        in_specs.append(st_spec)
        args.append(s0t)
    in_specs += [seq(GLA_DK_W), seq(GLA_DK_W), seq(GLA_DV_W), seq(GLA_DK_W), seq(GLA_DK_W),
                 seq(GLA_DV_W), pl.BlockSpec((1, DV), lambda b: (0, 0))]
    args += [r3(q), r3(k), r3(v), r3(laf), r3(lab), r3(sg), g_gla]
    return pl.pallas_call(
        functools.partial(_gla_kernel, seq_len=seq_len, has_s0=has_s0),
        grid=(batch,),
        in_specs=in_specs,
        out_specs=(seq(GLA_DV_W), st_spec),
        out_shape=(jax.ShapeDtypeStruct((batch, seq_len, GLA_DV_W), BF16),
                   jax.ShapeDtypeStruct((batch, 2, DV, GLA_DK_W), F32)),
        scratch_shapes=[
            pltpu.VMEM((2, seq_len, GLA_DK_W), BF16),
            pltpu.VMEM((2, seq_len, GLA_DK_W), BF16),
            pltpu.VMEM((2, seq_len, GLA_DK_W), BF16),
            pltpu.VMEM((2, seq_len // CHUNK, GLA_DK_W), F32),
            pltpu.VMEM((2, seq_len, GLA_DV_W), F32),
            pltpu.VMEM((2, DV, GLA_DK_W), F32),
        ],
        compiler_params=pltpu.CompilerParams(
            dimension_semantics=("arbitrary",), vmem_limit_bytes=VMEM_LIMIT),
        name="gla",
    )(*args)


FF_CHUNK = 256


def _tail_kernel(*refs, has_pe, seg, tm):
    if has_pe:
        x_ref, pe_ref, *refs = refs
    else:
        x_ref, *refs = refs
    (yf_ref, og_ref, mod_ref, wo_ref, gpm_ref, gpf_ref, gqf_ref,
     wup_ref, cw_ref, cb_ref, wdn_ref, out_ref, x1_s, a_s) = refs
    mod = mod_ref[0]
    x = x_ref[...]
    if has_pe:
        x = x + pe_ref[...]
    y = _dot(yf_ref[...], wo_ref[0:FOURIER_W, :]) + _dot(og_ref[...], wo_ref[FOURIER_W:, :])
    gate_m = mod[:, 2 * D_MODEL:3 * D_MODEL]
    x1 = x + gate_m * _rms(y, gpm_ref[...])
    x1_s[...] = x1
    shift_f = mod[:, 3 * D_MODEL:4 * D_MODEL]
    scale_f = mod[:, 4 * D_MODEL:5 * D_MODEL]
    h = (_rms(x1, gpf_ref[...]) * (1.0 + scale_f) + shift_f).astype(BF16)

    pos = lax.broadcasted_iota(jnp.int32, (tm, 1), 0) % seg
    first = pos == 0
    last = pos == seg - 1

    def conv_cols(c0):
        cols = slice(c0, c0 + FF_CHUNK)
        u = _dot(h, wup_ref[:, cols])
        u_prev = jnp.where(first, 0.0, pltpu.roll(u, 1, 0))
        u_next = jnp.where(last, 0.0, pltpu.roll(u, tm - 1, 0))
        return (u_prev * cw_ref[0:1, cols] + u * cw_ref[1:2, cols] + u_next * cw_ref[2:3, cols]
                + cb_ref[:, cols])

    for c in range(D_FF // FF_CHUNK):
        val = conv_cols(c * FF_CHUNK)
        gate = conv_cols(D_FF + c * FF_CHUNK)
        a_s[:, c * FF_CHUNK:(c + 1) * FF_CHUNK] = (_silu(gate) * val).astype(BF16)

    y2 = _dot(a_s[...], wdn_ref[...])
    gate_f = mod[:, 5 * D_MODEL:6 * D_MODEL]
    out_ref[...] = x1_s[...] + gate_f * _rms(y2, gqf_ref[...])


def _tail(x, pe, yf, og, mod, w_out, gpm, gpf, gqf, w_up, cw, cb, w_down, seq_len, seg, tm):
    n = x.shape[0]
    has_pe = pe is not None
    per_batch = mod.shape[0] > 1
    mod_idx = (lambda i: ((i * tm) // seq_len, 0, 0)) if per_batch else (lambda i: (0, 0, 0))
    row = lambda w: pl.BlockSpec((tm, w), lambda i: (i, 0))
    full = lambda a: pl.BlockSpec(a.shape, lambda i: (0,) * a.ndim, pipeline_mode=pl.Buffered(1))
    in_specs = [row(D_MODEL)]
    args = [x]
    if has_pe:
        in_specs.append(pl.BlockSpec((tm, D_MODEL), lambda i: (i % (seq_len // tm), 0)))
        args.append(pe)
    in_specs += [
        row(FOURIER_W), row(GLA_DV_W), pl.BlockSpec((1, 1, 6 * D_MODEL), mod_idx),
        full(w_out), full(gpm), full(gpf), full(gqf), full(w_up), full(cw), full(cb), full(w_down),
    ]
    args += [yf, og, mod, w_out, gpm, gpf, gqf, w_up, cw, cb, w_down]
    return pl.pallas_call(
        functools.partial(_tail_kernel, has_pe=has_pe, seg=seg, tm=tm),
        grid=(n // tm,),
        in_specs=in_specs,
        out_specs=row(D_MODEL),
        out_shape=jax.ShapeDtypeStruct((n, D_MODEL), F32),
        scratch_shapes=[
            pltpu.VMEM((tm, D_MODEL), F32),
            pltpu.VMEM((tm, D_FF), BF16),
        ],
        compiler_params=pltpu.CompilerParams(
            dimension_semantics=("arbitrary",), vmem_limit_bytes=VMEM_LIMIT),
        name="tail",
    )(*args)


def _grid_pos_embed(rows, d):
    quarter = d // 4
    omega = 1.0 / (POS_BASE ** (jnp.arange(quarter, dtype=F32) / quarter))
    er = jnp.arange(rows, dtype=F32)[:, None] * omega
    ec = jnp.arange(GRID_W, dtype=F32)[:, None] * omega
    pr = jnp.concatenate([jnp.sin(er), jnp.cos(er)], axis=-1)
    pc = jnp.concatenate([jnp.sin(ec), jnp.cos(ec)], axis=-1)
    pe = jnp.concatenate([jnp.broadcast_to(pr[:, None], (rows, GRID_W, d // 2)),
                          jnp.broadcast_to(pc[None], (rows, GRID_W, d // 2))], axis=-1)
    return pe.reshape(rows * GRID_W, d)


def _dft_tables(n, scale):
    idx = jnp.arange(n, dtype=jnp.int32)
    ang = ((idx[:, None] * idx[None, :]) % n).astype(F32) * np.float32(2.0 * np.pi / n)
    return jnp.cos(ang) * np.float32(scale), jnp.sin(ang) * np.float32(scale)


def _seqdft_tables(seq_len):
    radix = seq_len // DFT_P
    c, s = _dft_tables(DFT_P, seq_len ** -0.5)
    tables = (c.astype(BF16), (-s).astype(BF16))
    if radix > 1:
        k = (jnp.arange(DFT_P, dtype=jnp.int32)[:, None, None]
             + DFT_P * jnp.arange(radix, dtype=jnp.int32)[None, :, None])
        r = jnp.arange(radix, dtype=jnp.int32)[None, None, :]
        ang = ((k * r) % seq_len).astype(F32) * np.float32(2.0 * np.pi / seq_len)
        ang = ang.reshape(DFT_P, radix * radix)
        tables += (jnp.cos(ang), jnp.sin(ang))
    return tables


def _layer_weights(l, w_in, w_gate_f, b_gate_f, w_gate_b, b_gate_b, w_out, w_up, conv_w, conv_b,
                   w_down):
    wi = w_in[l]
    w_in_r = jnp.concatenate(
        [wi[:, :S_V], wi[:, S_AB:], wi[:, S_V:S_AB],
         jnp.zeros((D_MODEL, LANE - 2 * GATE_RANK), F32)], axis=1).astype(BF16)
    wg = jnp.zeros((LANE, 2 * GLA_DK_W), F32)
    wg = wg.at[0:GATE_RANK, 0:GLA_DK_W].set(w_gate_f[l])
    wg = wg.at[GATE_RANK:2 * GATE_RANK, GLA_DK_W:].set(w_gate_b[l])
    bg = jnp.concatenate([b_gate_f[l], b_gate_b[l]])[None]

    return dict(
        w_in_r=w_in_r, wg=wg.astype(BF16), bg=bg, w_out=w_out[l].astype(BF16),
        w_up=w_up[l].astype(BF16), cw=conv_w[l], cb=conv_b[l][None],
        w_down=w_down[l].astype(BF16))


def _stream_layer(x, pe, mod, s0t, lw, gains, dft, batch, seq_len, seg, tm_in, nb_dft, tm_tail):
    g_pre_mix, g_post_mix, g_pre_ffn, g_post_ffn, g_gla = gains
    cc, seq_tables = dft
    ab, q, k, v, laf, lab, sg = _inproj(x, pe, mod, g_pre_mix, lw['w_in_r'], cc, lw['wg'],
                                        lw['bg'], seq_len, tm_in)
    yf = _seqdft(ab, seq_tables, batch, seq_len, nb_dft)
    og, s_t = _gla(q, k, v, laf, lab, sg, g_gla, s0t, batch, seq_len)
    x_new = _tail(x, pe, yf, og.reshape(batch * seq_len, GLA_DV_W), mod, lw['w_out'], g_post_mix,
                  g_pre_ffn, g_post_ffn, lw['w_up'], lw['cw'], lw['cb'], lw['w_down'],
                  seq_len, seg, tm_tail)
    return x_new, s_t


def kernel(x_prompt, x_sample, state_gla, c, c_ctx, g_pre_mix, g_post_mix, g_pre_ffn, g_post_ffn,
           w_ada, b_ada, w_in, w_gate_f, b_gate_f, w_gate_b, b_gate_b, g_gla, w_out, w_up, conv_w,
           conv_b, w_down):
    pb, pt, _ = x_prompt.shape
    sb, s_t, _ = x_sample.shape
    rows = s_t // GRID_W

    cv = jnp.concatenate([c_ctx[None], c, jnp.zeros((MOD_ROWS - 1 - sb, D_MODEL), F32)], axis=0)
    mod_all = _ada(cv, w_ada, b_ada)

    cc_c, cc_s = _dft_tables(FOURIER_HD, FOURIER_HD ** -0.5)
    cc = jnp.concatenate([cc_c, cc_s], axis=1).astype(BF16)
    dft_p = (cc, _seqdft_tables(pt))
    dft_s = (cc, _seqdft_tables(s_t))
    pe = _grid_pos_embed(rows, D_MODEL)

    xp = x_prompt.reshape(pb * pt, D_MODEL)
    xs = x_sample.reshape(sb * s_t, D_MODEL)
    ctx_states = []
    for l in range(DEPTH):
        lw = _layer_weights(l, w_in, w_gate_f, b_gate_f, w_gate_b, b_gate_b, w_out, w_up, conv_w,
                            conv_b, w_down)
        gains = (g_pre_mix[l][None], g_post_mix[l][None], g_pre_ffn[l][None],
                 g_post_ffn[l][None], g_gla[l][None])
        mod_p = mod_all[l, 0:1][:, None]
        mod_s = mod_all[l, 1:1 + sb][:, None]
        xp, st_p = _stream_layer(xp, None, mod_p, None, lw, gains, dft_p, pb, pt, pt,
                                 512, 4, 512)
        ctx_states.append(
            st_p.reshape(pb, 2, DV, GLA_HEADS, DK).transpose(0, 1, 3, 4, 2))
        s0t = state_gla[:, l].transpose(0, 1, 4, 2, 3).reshape(sb, 2, DV, GLA_DK_W)
        xs, _ = _stream_layer(xs, pe if l == 0 else None, mod_s, s0t, lw, gains, dft_s, sb, s_t,
                              GRID_W, 512, 1, 512)
    new_state = jnp.stack(ctx_states, axis=1)
    return (xp.reshape(pb, pt, D_MODEL), xs.reshape(sb, s_t, D_MODEL), new_state)
```

```python
import functools

import numpy as np
import jax
import jax.numpy as jnp
from jax import lax
from jax.experimental import pallas as pl
from jax.experimental.pallas import tpu as pltpu

D_MODEL = 1024
DEPTH = 2
GRID_W = 64
FOURIER_W = 512
FOURIER_HEADS = 4
FOURIER_HD = FOURIER_W // FOURIER_HEADS
GLA_DV_W = 512
GLA_DK_W = 256
GLA_HEADS = 4
DV = GLA_DV_W // GLA_HEADS
DK = GLA_DK_W // GLA_HEADS
GATE_RANK = 16
GATE_TEMP = 16.0
CHUNK = 64
D_FF = 11 * D_MODEL // 4
EPS = 1e-6
POS_BASE = 10000.0
S_F = FOURIER_W
S_Q = S_F + GLA_DK_W
S_K = S_Q + GLA_DK_W
S_V = S_K + GLA_DV_W
S_AF = S_V + GATE_RANK
S_AB = S_AF + GATE_RANK
IN_COLS = S_AB + GLA_DV_W

C_F = 0
C_Q = C_F + FOURIER_W
C_K = C_Q + GLA_DK_W
C_V = C_K + GLA_DK_W
C_G = C_V + GLA_DV_W
C_A = C_G + GLA_DV_W
LANE = 128
IN_COLS_PAD = C_A + LANE
MOD_ROWS = 8
BLK = 256

VMEM_LIMIT = 56 * 1024 * 1024

F32 = jnp.float32
BF16 = jnp.bfloat16


def _silu(x):
    return x * jax.nn.sigmoid(x)


def _rms(x, g):
    return x * lax.rsqrt(jnp.mean(x * x, axis=-1, keepdims=True) + EPS) * g


def _dot(a, b):
    return jnp.dot(a, b, preferred_element_type=F32)


def _dot_nt(a, b):
    return lax.dot_general(a, b, (((1,), (1,)), ((), ())), preferred_element_type=F32)


def _ada_kernel(cv_ref, w_ref, b_ref, o_ref):
    s = _silu(cv_ref[...]).astype(BF16)
    o_ref[0] = _dot(s, w_ref[0].astype(BF16)) + b_ref[0]


def _ada(cv, w_ada, b_ada):
    tn = 1536
    n_out = w_ada.shape[-1]
    return pl.pallas_call(
        _ada_kernel,
        grid=(DEPTH, n_out // tn),
        in_specs=[
            pl.BlockSpec((MOD_ROWS, D_MODEL), lambda l, n: (0, 0)),
            pl.BlockSpec((1, D_MODEL, tn), lambda l, n: (l, 0, n)),
            pl.BlockSpec((1, 1, tn), lambda l, n: (l, 0, n)),
        ],
        out_specs=pl.BlockSpec((1, MOD_ROWS, tn), lambda l, n: (l, 0, n)),
        out_shape=jax.ShapeDtypeStruct((DEPTH, MOD_ROWS, n_out), F32),
        compiler_params=pltpu.CompilerParams(
            dimension_semantics=("arbitrary", "arbitrary"), vmem_limit_bytes=VMEM_LIMIT),
        name="ada",
    )(cv, w_ada, b_ada.reshape(DEPTH, 1, n_out))


def _inproj_kernel(*refs, has_pe, radix, tm):
    if has_pe:
        x_ref, pe_ref, *refs = refs
    else:
        x_ref, *refs = refs
    if radix > 1:
        *refs, ab_s = refs
    (mod_ref, g_ref, w_ref, cc_ref, wg_ref, bg_ref,
     ab_ref, q_ref, k_ref, v_ref, laf_ref, lab_ref, sg_ref) = refs
    x = x_ref[...]
    if has_pe:
        x = x + pe_ref[...]
    mod = mod_ref[0]
    shift = mod[:, 0:D_MODEL]
    scale = mod[:, D_MODEL:2 * D_MODEL]
    h = (_rms(x, g_ref[...]) * (1.0 + scale) + shift).astype(BF16)
    z = _dot(h, w_ref[...])
    zf = z[:, C_F:C_Q].astype(BF16)
    for hh in range(FOURIER_HEADS):
        sl = slice(hh * FOURIER_HD, (hh + 1) * FOURIER_HD)
        cs = _dot(zf[:, sl], cc_ref[...])
        if radix == 1:
            ab_ref[0, :, sl] = cs[:, :FOURIER_HD].astype(BF16)
            ab_ref[1, :, sl] = cs[:, FOURIER_HD:].astype(BF16)
        else:
            ab_s[0, hh] = cs[:, :FOURIER_HD]
            ab_s[1, hh] = cs[:, FOURIER_HD:]
    if radix > 1:
        for r in range(radix):
            for part in range(2):
                for hh in range(FOURIER_HEADS):
                    c0 = r * FOURIER_W + hh * FOURIER_HD
                    ab_ref[part, :, c0:c0 + FOURIER_HD] = (
                        ab_s[part, hh, pl.ds(r, tm // radix, stride=radix), :].astype(BF16))
    q_ref[...] = z[:, C_Q:C_K] * (DK ** -0.5)
    k_ref[...] = z[:, C_K:C_V]
    v_ref[...] = z[:, C_V:C_G].astype(BF16)
    sg_ref[...] = _silu(z[:, C_G:C_A]).astype(BF16)
    xg = _dot(z[:, C_A:IN_COLS_PAD].astype(BF16), wg_ref[...]) + bg_ref[...]
    la = (jnp.minimum(xg, 0.0) - jnp.log1p(jnp.exp(-jnp.abs(xg)))) * (1.0 / GATE_TEMP)
    laf_ref[...] = la[:, :GLA_DK_W]
    lab_ref[...] = la[:, GLA_DK_W:]


def _inproj(x, pe, mod, g, w_in_r, cc, wg, bg, seq_len, tm):
    n = x.shape[0]
    radix = seq_len // DFT_P
    has_pe = pe is not None
    per_batch = mod.shape[0] > 1
    mod_idx = (lambda i: ((i * tm) // seq_len, 0, 0)) if per_batch else (lambda i: (0, 0, 0))
    row = lambda w: pl.BlockSpec((tm, w), lambda i: (i, 0))
    full = lambda a: pl.BlockSpec(a.shape, lambda i: (0,) * a.ndim)
    in_specs = [row(D_MODEL)]
    args = [x]
    if has_pe:
        in_specs.append(pl.BlockSpec((tm, D_MODEL), lambda i: (i % (seq_len // tm), 0)))
        args.append(pe)
    in_specs += [pl.BlockSpec((1, 1, 6 * D_MODEL), mod_idx), full(g), full(w_in_r), full(cc),
                 full(wg), full(bg)]
    args += [mod, g, w_in_r, cc, wg, bg]
    out_shape = (
        jax.ShapeDtypeStruct((2, n // radix, radix * FOURIER_W), BF16),
        jax.ShapeDtypeStruct((n, GLA_DK_W), F32),
        jax.ShapeDtypeStruct((n, GLA_DK_W), F32),
        jax.ShapeDtypeStruct((n, GLA_DV_W), BF16),
        jax.ShapeDtypeStruct((n, GLA_DK_W), F32),
        jax.ShapeDtypeStruct((n, GLA_DK_W), F32),
        jax.ShapeDtypeStruct((n, GLA_DV_W), BF16),
    )
    out_specs = (
        pl.BlockSpec((2, tm // radix, radix * FOURIER_W), lambda i: (0, i, 0)),
        row(GLA_DK_W), row(GLA_DK_W), row(GLA_DV_W), row(GLA_DK_W), row(GLA_DK_W), row(GLA_DV_W),
    )
    scratch = [pltpu.VMEM((2, FOURIER_HEADS, tm, FOURIER_HD), F32)] if radix > 1 else []
    return pl.pallas_call(
        functools.partial(_inproj_kernel, has_pe=has_pe, radix=radix, tm=tm),
        grid=(n // tm,),
        in_specs=in_specs,
        out_specs=out_specs,
        out_shape=out_shape,
        scratch_shapes=scratch,
        compiler_params=pltpu.CompilerParams(
            dimension_semantics=("arbitrary",), vmem_limit_bytes=VMEM_LIMIT),
        name="inproj",
    )(*args)


DFT_P = 256


def _seqdft_kernel(*refs, radix, nb):
    if radix > 1:
        c_ref, sn_ref, twc_ref, tws_ref, a_ref, b_ref, o_ref = refs
    else:
        c_ref, sn_ref, a_ref, b_ref, o_ref = refs
    c = c_ref[...]
    sn = sn_ref[...]
    for bi in range(nb):
        a = a_ref[bi]
        b = b_ref[bi]
        g_re = _dot(c, a) + _dot(sn, b)
        if radix == 1:
            o_ref[bi, 0] = g_re.astype(BF16)
            continue
        g_im = _dot(sn, a) - _dot(c, b)
        for k2 in range(radix):
            acc = g_re[:, 0:FOURIER_W]
            for r in range(1, radix):
                j = k2 * radix + r
                cols = slice(r * FOURIER_W, (r + 1) * FOURIER_W)
                acc = acc + twc_ref[:, j:j + 1] * g_re[:, cols] + tws_ref[:, j:j + 1] * g_im[:, cols]
            o_ref[bi, k2] = acc.astype(BF16)


def _seqdft(ab, tables, batch, seq_len, nb):
    radix = seq_len // DFT_P
    width = radix * FOURIER_W
    ab4 = ab.reshape(2, batch, DFT_P, width)
    full = lambda a: pl.BlockSpec(a.shape, lambda i: (0,) * a.ndim)
    out = pl.pallas_call(
        functools.partial(_seqdft_kernel, radix=radix, nb=nb),
        grid=(batch // nb,),
        in_specs=[full(t) for t in tables] + [
            pl.BlockSpec((None, nb, DFT_P, width), lambda i: (0, i, 0, 0)),
            pl.BlockSpec((None, nb, DFT_P, width), lambda i: (1, i, 0, 0)),
        ],
        out_specs=pl.BlockSpec((nb, radix, DFT_P, FOURIER_W), lambda i: (i, 0, 0, 0)),
        out_shape=jax.ShapeDtypeStruct((batch, radix, DFT_P, FOURIER_W), BF16),
        compiler_params=pltpu.CompilerParams(
            dimension_semantics=("arbitrary",), vmem_limit_bytes=VMEM_LIMIT),
        name="seqdft",
    )(*tables, ab4, ab4)
    return out.reshape(batch * seq_len, FOURIER_W)


def _gla_kernel(*refs, seq_len, has_s0):
    if has_s0:
        s0_ref, *refs = refs
    (q_ref, k_ref, v_ref, laf_ref, lab_ref, sg_ref, g_ref,
     og_ref, sT_ref,
     qt_s, kt_s, ke_s, dec_s, o_s, st_s) = refs
    nc = seq_len // CHUNK
    la_refs = (laf_ref, lab_ref)

    ri = lax.broadcasted_iota(jnp.int32, (BLK, BLK), 0)
    ci = lax.broadcasted_iota(jnp.int32, (BLK, BLK), 1)
    same = (ri // CHUNK) == (ci // CHUNK)
    tri = (jnp.where(same & (ci <= ri), 1.0, 0.0).astype(BF16),
           jnp.where(same & (ci >= ri), 1.0, 0.0).astype(BF16))

    for blk in range(seq_len // BLK):
        rows = slice(blk * BLK, (blk + 1) * BLK)
        q = q_ref[rows, :]
        k = k_ref[rows, :]
        for d in range(2):
            la = la_refs[d][rows, :]
            hi = la.astype(BF16)
            lo = (la - hi.astype(F32)).astype(BF16)
            b = _dot(tri[d], hi) + _dot(tri[d], lo)
            edge = CHUNK - 1 if d == 0 else 0
            tot = jnp.concatenate(
                [jnp.broadcast_to(b[cc * CHUNK + edge:cc * CHUNK + edge + 1, :], (CHUNK, GLA_DK_W))
                 for cc in range(BLK // CHUNK)], axis=0)
            qt_s[d, rows, :] = (q * jnp.exp(b)).astype(BF16)
            kt_s[d, rows, :] = (k * jnp.exp(-b)).astype(BF16)
            ke_s[d, rows, :] = (k * jnp.exp(tot - b)).astype(BF16)
            for cc in range(BLK // CHUNK):
                c = blk * (BLK // CHUNK) + cc
                dec_s[d, c:c + 1, :] = jnp.exp(tot[cc * CHUNK:cc * CHUNK + 1, :])
    for d in range(2):
        if has_s0:
            st_s[d] = s0_ref[d]
        else:
            st_s[d] = jnp.zeros((DV, GLA_DK_W), F32)

    lane_head = lax.broadcasted_iota(jnp.int32, (1, GLA_DK_W), 1) // DK
    head_mask = [jnp.where(lane_head == h, 1.0, 0.0).astype(BF16) for h in range(GLA_HEADS)]
    ti = lax.broadcasted_iota(jnp.int32, (CHUNK, CHUNK), 0)
    tj = lax.broadcasted_iota(jnp.int32, (CHUNK, CHUNK), 1)
    causal = (tj <= ti, tj >= ti)

    def pair_step(chunks):
        rows = []
        for c in chunks:
            r0 = c * CHUNK
            rows.append(pl.ds(r0 if isinstance(r0, int) else pl.multiple_of(r0, CHUNK), CHUNK))
        v_c = [v_ref[rows[d], :] for d in range(2)]
        res = []
        for d in range(2):
            qt_c = qt_s[d, rows[d], :]
            q_stack = jnp.concatenate([qt_c * head_mask[h] for h in range(GLA_HEADS)], axis=0)
            w = jnp.concatenate([st_s[d].astype(BF16), kt_s[d, rows[d], :]], axis=0)
            res.append(_dot_nt(q_stack, w))
        kv = []
        for d in range(2):
            ke_c = ke_s[d, rows[d], :]
            k_bd = jnp.concatenate([ke_c * head_mask[h] for h in range(GLA_HEADS)], axis=0)
            v_stack = jnp.concatenate(
                [v_c[d][:, h * DV:(h + 1) * DV] for h in range(GLA_HEADS)], axis=0)
            v_stack_t = v_stack.astype(F32).T.astype(BF16)
            kv.append(_dot(v_stack_t, k_bd))
        for d in range(2):
            c = chunks[d]
            st_s[d] = dec_s[d, pl.ds(c, 1), :] * st_s[d] + kv[d]
        for d in range(2):
            for h in range(GLA_HEADS):
                hr = slice(h * CHUNK, (h + 1) * CHUNK)
                hv = slice(h * DV, (h + 1) * DV)
                att = jnp.where(causal[d], res[d][hr, DV:DV + CHUNK], 0.0).astype(BF16)
                o_s[d, rows[d], hv] = res[d][hr, 0:DV] + _dot(att, v_c[d][:, hv])

    if nc <= 4:
        for i in range(nc):
            pair_step((i, nc - 1 - i))
    else:
        def body(i, carry):
            pair_step((i, nc - 1 - i))
            return carry
        lax.fori_loop(0, nc, body, 0, unroll=4)

    g = g_ref[...]
    for h in range(GLA_HEADS):
        hv = slice(h * DV, (h + 1) * DV)
        o = o_s[0, :, hv] + o_s[1, :, hv]
        og_ref[:, hv] = (_rms(o, g) * sg_ref[:, hv].astype(F32)).astype(BF16)
    sT_ref[...] = st_s[...]


def _gla(q, k, v, laf, lab, sg, g_gla, s0t, batch, seq_len):
    has_s0 = s0t is not None
    r3 = lambda a: a.reshape(batch, seq_len, a.shape[-1])
    seq = lambda w: pl.BlockSpec((None, seq_len, w), lambda b: (b, 0, 0))
    st_spec = pl.BlockSpec((None, 2, DV, GLA_DK_W), lambda b: (b, 0, 0, 0))
    in_specs = []
    args = []
    if has_s0:
        in_specs.append(st_spec)
        args.append(s0t)
    in_specs += [seq(GLA_DK_W), seq(GLA_DK_W), seq(GLA_DV_W), seq(GLA_DK_W), seq(GLA_DK_W),
                 seq(GLA_DV_W), pl.BlockSpec((1, DV), lambda b: (0, 0))]
    args += [r3(q), r3(k), r3(v), r3(laf), r3(lab), r3(sg), g_gla]
    return pl.pallas_call(
        functools.partial(_gla_kernel, seq_len=seq_len, has_s0=has_s0),
        grid=(batch,),
        in_specs=in_specs,
        out_specs=(seq(GLA_DV_W), st_spec),
        out_shape=(jax.ShapeDtypeStruct((batch, seq_len, GLA_DV_W), BF16),
                   jax.ShapeDtypeStruct((batch, 2, DV, GLA_DK_W), F32)),
        scratch_shapes=[
            pltpu.VMEM((2, seq_len, GLA_DK_W), BF16),
            pltpu.VMEM((2, seq_len, GLA_DK_W), BF16),
            pltpu.VMEM((2, seq_len, GLA_DK_W), BF16),
            pltpu.VMEM((2, seq_len // CHUNK, GLA_DK_W), F32),
            pltpu.VMEM((2, seq_len, GLA_DV_W), F32),
            pltpu.VMEM((2, DV, GLA_DK_W), F32),
        ],
        compiler_params=pltpu.CompilerParams(
            dimension_semantics=("arbitrary",), vmem_limit_bytes=VMEM_LIMIT),
        name="gla",
    )(*args)


FF_CHUNK = 256


def _tail_kernel(*refs, has_pe, seg, tm):
    if has_pe:
        x_ref, pe_ref, *refs = refs
    else:
        x_ref, *refs = refs
    (yf_ref, og_ref, mod_ref, wo_ref, gpm_ref, gpf_ref, gqf_ref,
     wup_ref, cw_ref, cb_ref, wdn_ref, out_ref, x1_s, a_s) = refs
    mod = mod_ref[0]
    x = x_ref[...]
    if has_pe:
        x = x + pe_ref[...]
    y = _dot(yf_ref[...], wo_ref[0:FOURIER_W, :]) + _dot(og_ref[...], wo_ref[FOURIER_W:, :])
    gate_m = mod[:, 2 * D_MODEL:3 * D_MODEL]
    x1 = x + gate_m * _rms(y, gpm_ref[...])
    x1_s[...] = x1
    shift_f = mod[:, 3 * D_MODEL:4 * D_MODEL]
    scale_f = mod[:, 4 * D_MODEL:5 * D_MODEL]
    h = (_rms(x1, gpf_ref[...]) * (1.0 + scale_f) + shift_f).astype(BF16)

    pos = lax.broadcasted_iota(jnp.int32, (tm, 1), 0) % seg
    first = pos == 0
    last = pos == seg - 1

    def conv_cols(c0):
        cols = slice(c0, c0 + FF_CHUNK)
        u = _dot(h, wup_ref[:, cols])
        u_prev = jnp.where(first, 0.0, pltpu.roll(u, 1, 0))
        u_next = jnp.where(last, 0.0, pltpu.roll(u, tm - 1, 0))
        return (u_prev * cw_ref[0:1, cols] + u * cw_ref[1:2, cols] + u_next * cw_ref[2:3, cols]
                + cb_ref[:, cols])

    for c in range(D_FF // FF_CHUNK):
        val = conv_cols(c * FF_CHUNK)
        gate = conv_cols(D_FF + c * FF_CHUNK)
        a_s[:, c * FF_CHUNK:(c + 1) * FF_CHUNK] = (_silu(gate) * val).astype(BF16)

    y2 = _dot(a_s[...], wdn_ref[...])
    gate_f = mod[:, 5 * D_MODEL:6 * D_MODEL]
    out_ref[...] = x1_s[...] + gate_f * _rms(y2, gqf_ref[...])


def _tail(x, pe, yf, og, mod, w_out, gpm, gpf, gqf, w_up, cw, cb, w_down, seq_len, seg, tm):
    n = x.shape[0]
    has_pe = pe is not None
    per_batch = mod.shape[0] > 1
    mod_idx = (lambda i: ((i * tm) // seq_len, 0, 0)) if per_batch else (lambda i: (0, 0, 0))
    row = lambda w: pl.BlockSpec((tm, w), lambda i: (i, 0))
    full = lambda a: pl.BlockSpec(a.shape, lambda i: (0,) * a.ndim, pipeline_mode=pl.Buffered(1))
    in_specs = [row(D_MODEL)]
    args = [x]
    if has_pe:
        in_specs.append(pl.BlockSpec((tm, D_MODEL), lambda i: (i % (seq_len // tm), 0)))
        args.append(pe)
    in_specs += [
        row(FOURIER_W), row(GLA_DV_W), pl.BlockSpec((1, 1, 6 * D_MODEL), mod_idx),
        full(w_out), full(gpm), full(gpf), full(gqf), full(w_up), full(cw), full(cb), full(w_down),
    ]
    args += [yf, og, mod, w_out, gpm, gpf, gqf, w_up, cw, cb, w_down]
    return pl.pallas_call(
        functools.partial(_tail_kernel, has_pe=has_pe, seg=seg, tm=tm),
        grid=(n // tm,),
        in_specs=in_specs,
        out_specs=row(D_MODEL),
        out_shape=jax.ShapeDtypeStruct((n, D_MODEL), F32),
        scratch_shapes=[
            pltpu.VMEM((tm, D_MODEL), F32),
            pltpu.VMEM((tm, D_FF), BF16),
        ],
        compiler_params=pltpu.CompilerParams(
            dimension_semantics=("arbitrary",), vmem_limit_bytes=VMEM_LIMIT),
        name="tail",
    )(*args)


def _grid_pos_embed(rows, d):
    quarter = d // 4
    omega = 1.0 / (POS_BASE ** (jnp.arange(quarter, dtype=F32) / quarter))
    er = jnp.arange(rows, dtype=F32)[:, None] * omega
    ec = jnp.arange(GRID_W, dtype=F32)[:, None] * omega
    pr = jnp.concatenate([jnp.sin(er), jnp.cos(er)], axis=-1)
    pc = jnp.concatenate([jnp.sin(ec), jnp.cos(ec)], axis=-1)
    pe = jnp.concatenate([jnp.broadcast_to(pr[:, None], (rows, GRID_W, d // 2)),
                          jnp.broadcast_to(pc[None], (rows, GRID_W, d // 2))], axis=-1)
    return pe.reshape(rows * GRID_W, d)


def _dft_tables(n, scale):
    idx = jnp.arange(n, dtype=jnp.int32)
    ang = ((idx[:, None] * idx[None, :]) % n).astype(F32) * np.float32(2.0 * np.pi / n)
    return jnp.cos(ang) * np.float32(scale), jnp.sin(ang) * np.float32(scale)


def _seqdft_tables(seq_len):
    radix = seq_len // DFT_P
    c, s = _dft_tables(DFT_P, seq_len ** -0.5)
    tables = (c.astype(BF16), (-s).astype(BF16))
    if radix > 1:
        k = (jnp.arange(DFT_P, dtype=jnp.int32)[:, None, None]
             + DFT_P * jnp.arange(radix, dtype=jnp.int32)[None, :, None])
        r = jnp.arange(radix, dtype=jnp.int32)[None, None, :]
        ang = ((k * r) % seq_len).astype(F32) * np.float32(2.0 * np.pi / seq_len)
        ang = ang.reshape(DFT_P, radix * radix)
        tables += (jnp.cos(ang), jnp.sin(ang))
    return tables


def _layer_weights(l, w_in, w_gate_f, b_gate_f, w_gate_b, b_gate_b, w_out, w_up, conv_w, conv_b,
                   w_down):
    wi = w_in[l]
    w_in_r = jnp.concatenate(
        [wi[:, :S_V], wi[:, S_AB:], wi[:, S_V:S_AB],
         jnp.zeros((D_MODEL, LANE - 2 * GATE_RANK), F32)], axis=1).astype(BF16)
    wg = jnp.zeros((LANE, 2 * GLA_DK_W), F32)
    wg = wg.at[0:GATE_RANK, 0:GLA_DK_W].set(w_gate_f[l])
    wg = wg.at[GATE_RANK:2 * GATE_RANK, GLA_DK_W:].set(w_gate_b[l])
    bg = jnp.concatenate([b_gate_f[l], b_gate_b[l]])[None]

    return dict(
        w_in_r=w_in_r, wg=wg.astype(BF16), bg=bg, w_out=w_out[l].astype(BF16),
        w_up=w_up[l].astype(BF16), cw=conv_w[l], cb=conv_b[l][None],
        w_down=w_down[l].astype(BF16))


def _stream_layer(x, pe, mod, s0t, lw, gains, dft, batch, seq_len, seg, tm_in, nb_dft, tm_tail):
    g_pre_mix, g_post_mix, g_pre_ffn, g_post_ffn, g_gla = gains
    cc, seq_tables = dft
    ab, q, k, v, laf, lab, sg = _inproj(x, pe, mod, g_pre_mix, lw['w_in_r'], cc, lw['wg'],
                                        lw['bg'], seq_len, tm_in)
    yf = _seqdft(ab, seq_tables, batch, seq_len, nb_dft)
    og, s_t = _gla(q, k, v, laf, lab, sg, g_gla, s0t, batch, seq_len)
    x_new = _tail(x, pe, yf, og.reshape(batch * seq_len, GLA_DV_W), mod, lw['w_out'], g_post_mix,
                  g_pre_ffn, g_post_ffn, lw['w_up'], lw['cw'], lw['cb'], lw['w_down'],
                  seq_len, seg, tm_tail)
    return x_new, s_t


def kernel(x_prompt, x_sample, state_gla, c, c_ctx, g_pre_mix, g_post_mix, g_pre_ffn, g_post_ffn,
           w_ada, b_ada, w_in, w_gate_f, b_gate_f, w_gate_b, b_gate_b, g_gla, w_out, w_up, conv_w,
           conv_b, w_down):
    pb, pt, _ = x_prompt.shape
    sb, s_t, _ = x_sample.shape
    rows = s_t // GRID_W

    cv = jnp.concatenate([c_ctx[None], c, jnp.zeros((MOD_ROWS - 1 - sb, D_MODEL), F32)], axis=0)
    mod_all = _ada(cv, w_ada, b_ada)

    cc_c, cc_s = _dft_tables(FOURIER_HD, FOURIER_HD ** -0.5)
    cc = jnp.concatenate([cc_c, cc_s], axis=1).astype(BF16)
    dft_p = (cc, _seqdft_tables(pt))
    dft_s = (cc, _seqdft_tables(s_t))
    pe = _grid_pos_embed(rows, D_MODEL)

    xp = x_prompt.reshape(pb * pt, D_MODEL)
    xs = x_sample.reshape(sb * s_t, D_MODEL)
    ctx_states = []
    for l in range(DEPTH):
        lw = _layer_weights(l, w_in, w_gate_f, b_gate_f, w_gate_b, b_gate_b, w_out, w_up, conv_w,
                            conv_b, w_down)
        gains = (g_pre_mix[l][None], g_post_mix[l][None], g_pre_ffn[l][None],
                 g_post_ffn[l][None], g_gla[l][None])
        mod_p = mod_all[l, 0:1][:, None]
        mod_s = mod_all[l, 1:1 + sb][:, None]
        xp, st_p = _stream_layer(xp, None, mod_p, None, lw, gains, dft_p, pb, pt, pt,
                                 512, 4, 512)
        ctx_states.append(
            st_p.reshape(pb, 2, DV, GLA_HEADS, DK).transpose(0, 1, 3, 4, 2))
        s0t = state_gla[:, l].transpose(0, 1, 4, 2, 3).reshape(sb, 2, DV, GLA_DK_W)
        xs, _ = _stream_layer(xs, pe if l == 0 else None, mod_s, s0t, lw, gains, dft_s, sb, s_t,
                              GRID_W, 512, 1, 512)
    new_state = jnp.stack(ctx_states, axis=1)
    return (xp.reshape(pb, pt, D_MODEL), xs.reshape(sb, s_t, D_MODEL), new_state)
```

```python
import functools

import numpy as np
import jax
import jax.numpy as jnp
from jax import lax
from jax.experimental import pallas as pl
from jax.experimental.pallas import tpu as pltpu

D_MODEL = 1024
DEPTH = 2
GRID_W = 64
FOURIER_W = 512
FOURIER_HEADS = 4
FOURIER_HD = FOURIER_W // FOURIER_HEADS
GLA_DV_W = 512
GLA_DK_W = 256
GLA_HEADS = 4
DV = GLA_DV_W // GLA_HEADS
DK = GLA_DK_W // GLA_HEADS
GATE_RANK = 16
GATE_TEMP = 16.0
CHUNK = 64
D_FF = 11 * D_MODEL // 4
EPS = 1e-6
POS_BASE = 10000.0
S_F = FOURIER_W
S_Q = S_F + GLA_DK_W
S_K = S_Q + GLA_DK_W
S_V = S_K + GLA_DV_W
S_AF = S_V + GATE_RANK
S_AB = S_AF + GATE_RANK
IN_COLS = S_AB + GLA_DV_W

C_F = 0
C_Q = C_F + FOURIER_W
C_K = C_Q + GLA_DK_W
C_V = C_K + GLA_DK_W
C_G = C_V + GLA_DV_W
C_A = C_G + GLA_DV_W
LANE = 128
IN_COLS_PAD = C_A + LANE
MOD_ROWS = 8
BLK = 256

VMEM_LIMIT = 56 * 1024 * 1024

F32 = jnp.float32
BF16 = jnp.bfloat16


def _silu(x):
    return x * jax.nn.sigmoid(x)


def _rms(x, g):
    return x * lax.rsqrt(jnp.mean(x * x, axis=-1, keepdims=True) + EPS) * g


def _dot(a, b):
    return jnp.dot(a, b, preferred_element_type=F32)


def _dot_nt(a, b):
    return lax.dot_general(a, b, (((1,), (1,)), ((), ())), preferred_element_type=F32)


def _add_pos(x, pr_ref, pc_ref):
    half = D_MODEL // 2
    reps = x.shape[0] // GRID_W
    row_part = jnp.concatenate(
        [jnp.broadcast_to(pr_ref[j:j + 1, :], (GRID_W, half)) for j in range(reps)], axis=0)
    col_part = jnp.concatenate([pc_ref[...]] * reps, axis=0)
    return x + jnp.concatenate([row_part, col_part], axis=1)


def _pos_specs(pe, seq_len, tm):
    pr, pc = pe
    return [pl.BlockSpec((tm // GRID_W, pr.shape[1]), lambda i: (i % (seq_len // tm), 0)),
            pl.BlockSpec(pc.shape, lambda i: (0, 0))]


def _ada_kernel(cv_ref, w_ref, b_ref, o_ref):
    s = _silu(cv_ref[...]).astype(BF16)
    o_ref[0] = _dot(s, w_ref[0].astype(BF16)) + b_ref[0]


def _ada(cv, w_ada, b_ada):
    tn = 1536
    n_out = w_ada.shape[-1]
    return pl.pallas_call(
        _ada_kernel,
        grid=(DEPTH, n_out // tn),
        in_specs=[
            pl.BlockSpec((MOD_ROWS, D_MODEL), lambda l, n: (0, 0)),
            pl.BlockSpec((1, D_MODEL, tn), lambda l, n: (l, 0, n)),
            pl.BlockSpec((1, 1, tn), lambda l, n: (l, 0, n)),
        ],
        out_specs=pl.BlockSpec((1, MOD_ROWS, tn), lambda l, n: (l, 0, n)),
        out_shape=jax.ShapeDtypeStruct((DEPTH, MOD_ROWS, n_out), F32),
        compiler_params=pltpu.CompilerParams(
            dimension_semantics=("arbitrary", "arbitrary"), vmem_limit_bytes=VMEM_LIMIT),
        name="ada",
    )(cv, w_ada, b_ada.reshape(DEPTH, 1, n_out))


def _inproj_kernel(*refs, has_pe, radix, tm):
    if has_pe:
        x_ref, pr_ref, pc_ref, *refs = refs
    else:
        x_ref, *refs = refs
    if radix > 1:
        *refs, ab_s = refs
    (mod_ref, g_ref, w_ref, cc_ref, wg_ref, bg_ref,
     ab_ref, q_ref, k_ref, v_ref, laf_ref, lab_ref, sg_ref) = refs
    x = x_ref[...]
    if has_pe:
        x = _add_pos(x, pr_ref, pc_ref)
    mod = mod_ref[0]
    shift = mod[:, 0:D_MODEL]
    scale = mod[:, D_MODEL:2 * D_MODEL]
    h = (_rms(x, g_ref[...]) * (1.0 + scale) + shift).astype(BF16)
    z = _dot(h, w_ref[...])
    zf = z[:, C_F:C_Q].astype(BF16)
    for hh in range(FOURIER_HEADS):
        sl = slice(hh * FOURIER_HD, (hh + 1) * FOURIER_HD)
        cs = _dot(zf[:, sl], cc_ref[...])
        if radix == 1:
            ab_ref[0, :, sl] = cs[:, :FOURIER_HD].astype(BF16)
            ab_ref[1, :, sl] = cs[:, FOURIER_HD:].astype(BF16)
        else:
            ab_s[0, hh] = cs[:, :FOURIER_HD]
            ab_s[1, hh] = cs[:, FOURIER_HD:]
    if radix > 1:
        for r in range(radix):
            for part in range(2):
                for hh in range(FOURIER_HEADS):
                    c0 = r * FOURIER_W + hh * FOURIER_HD
                    ab_ref[part, :, c0:c0 + FOURIER_HD] = (
                        ab_s[part, hh, pl.ds(r, tm // radix, stride=radix), :].astype(BF16))
    q_ref[...] = z[:, C_Q:C_K] * (DK ** -0.5)
    k_ref[...] = z[:, C_K:C_V]
    v_ref[...] = z[:, C_V:C_G].astype(BF16)
    sg_ref[...] = _silu(z[:, C_G:C_A]).astype(BF16)
    xg = _dot(z[:, C_A:IN_COLS_PAD].astype(BF16), wg_ref[...]) + bg_ref[...]
    la = (jnp.minimum(xg, 0.0) - jnp.log1p(jnp.exp(-jnp.abs(xg)))) * (1.0 / GATE_TEMP)
    laf_ref[...] = la[:, :GLA_DK_W]
    lab_ref[...] = la[:, GLA_DK_W:]


def _inproj(x, pe, mod, g, w_in_r, cc, wg, bg, seq_len, tm):
    n = x.shape[0]
    radix = seq_len // _dft_len(seq_len)
    has_pe = pe is not None
    per_batch = mod.shape[0] > 1
    mod_idx = (lambda i: ((i * tm) // seq_len, 0, 0)) if per_batch else (lambda i: (0, 0, 0))
    row = lambda w: pl.BlockSpec((tm, w), lambda i: (i, 0))
    full = lambda a: pl.BlockSpec(a.shape, lambda i: (0,) * a.ndim)
    in_specs = [row(D_MODEL)]
    args = [x]
    if has_pe:
        in_specs += _pos_specs(pe, seq_len, tm)
        args += list(pe)
    in_specs += [pl.BlockSpec((1, 1, 6 * D_MODEL), mod_idx), full(g), full(w_in_r), full(cc),
                 full(wg), full(bg)]
    args += [mod, g, w_in_r, cc, wg, bg]
    out_shape = (
        jax.ShapeDtypeStruct((2, n // radix, radix * FOURIER_W), BF16),
        jax.ShapeDtypeStruct((n, GLA_DK_W), F32),
        jax.ShapeDtypeStruct((n, GLA_DK_W), F32),
        jax.ShapeDtypeStruct((n, GLA_DV_W), BF16),
        jax.ShapeDtypeStruct((n, GLA_DK_W), F32),
        jax.ShapeDtypeStruct((n, GLA_DK_W), F32),
        jax.ShapeDtypeStruct((n, GLA_DV_W), BF16),
    )
    out_specs = (
        pl.BlockSpec((2, tm // radix, radix * FOURIER_W), lambda i: (0, i, 0)),
        row(GLA_DK_W), row(GLA_DK_W), row(GLA_DV_W), row(GLA_DK_W), row(GLA_DK_W), row(GLA_DV_W),
    )
    scratch = [pltpu.VMEM((2, FOURIER_HEADS, tm, FOURIER_HD), F32)] if radix > 1 else []
    return pl.pallas_call(
        functools.partial(_inproj_kernel, has_pe=has_pe, radix=radix, tm=tm),
        grid=(n // tm,),
        in_specs=in_specs,
        out_specs=out_specs,
        out_shape=out_shape,
        scratch_shapes=scratch,
        compiler_params=pltpu.CompilerParams(
            dimension_semantics=("arbitrary",), vmem_limit_bytes=VMEM_LIMIT),
        name="inproj",
    )(*args)


DFT_MAX = 512


def _dft_len(seq_len):
    return min(seq_len, DFT_MAX)


def _seqdft_kernel(*refs, radix, nb):
    if radix > 1:
        c_ref, sn_ref, twc_ref, tws_ref, a_ref, b_ref, o_ref = refs
    else:
        c_ref, sn_ref, a_ref, b_ref, o_ref = refs
    c = c_ref[...]
    sn = sn_ref[...]
    for bi in range(nb):
        a = a_ref[bi]
        b = b_ref[bi]
        g_re = _dot(c, a) + _dot(sn, b)
        if radix == 1:
            o_ref[bi, 0] = g_re.astype(BF16)
            continue
        g_im = _dot(sn, a) - _dot(c, b)
        for k2 in range(radix):
            acc = g_re[:, 0:FOURIER_W]
            for r in range(1, radix):
                j = k2 * radix + r
                cols = slice(r * FOURIER_W, (r + 1) * FOURIER_W)
                acc = acc + twc_ref[:, j:j + 1] * g_re[:, cols] + tws_ref[:, j:j + 1] * g_im[:, cols]
            o_ref[bi, k2] = acc.astype(BF16)


def _seqdft(ab, tables, batch, seq_len, nb):
    p = _dft_len(seq_len)
    radix = seq_len // p
    width = radix * FOURIER_W
    ab4 = ab.reshape(2, batch, p, width)
    full = lambda a: pl.BlockSpec(a.shape, lambda i: (0,) * a.ndim)
    out = pl.pallas_call(
        functools.partial(_seqdft_kernel, radix=radix, nb=nb),
        grid=(batch // nb,),
        in_specs=[full(t) for t in tables] + [
            pl.BlockSpec((None, nb, p, width), lambda i: (0, i, 0, 0)),
            pl.BlockSpec((None, nb, p, width), lambda i: (1, i, 0, 0)),
        ],
        out_specs=pl.BlockSpec((nb, radix, p, FOURIER_W), lambda i: (i, 0, 0, 0)),
        out_shape=jax.ShapeDtypeStruct((batch, radix, p, FOURIER_W), BF16),
        compiler_params=pltpu.CompilerParams(
            dimension_semantics=("arbitrary",), vmem_limit_bytes=VMEM_LIMIT),
        name="seqdft",
    )(*tables, ab4, ab4)
    return out.reshape(batch * seq_len, FOURIER_W)


def _gla_kernel(*refs, seq_len, has_s0, want_state):
    if has_s0:
        s0_ref, *refs = refs
    if want_state:
        _, *refs = refs
    q_ref, k_ref, v_ref, laf_ref, lab_ref, sg_ref, g_ref, og_ref, *refs = refs
    if want_state:
        sT_ref, *refs = refs
    qt_s, kt_s, ke_s, dec_s, o_s, st_s = refs
    nc = seq_len // CHUNK
    la_refs = (laf_ref, lab_ref)

    ri = lax.broadcasted_iota(jnp.int32, (BLK, BLK), 0)
    ci = lax.broadcasted_iota(jnp.int32, (BLK, BLK), 1)
    same = (ri // CHUNK) == (ci // CHUNK)
    tri = (jnp.where(same & (ci <= ri), 1.0, 0.0).astype(BF16),
           jnp.where(same & (ci >= ri), 1.0, 0.0).astype(BF16))

    for blk in range(seq_len // BLK):
        rows = slice(blk * BLK, (blk + 1) * BLK)
        q = q_ref[rows, :]
        k = k_ref[rows, :]
        for d in range(2):
            la = la_refs[d][rows, :]
            hi = la.astype(BF16)
            lo = (la - hi.astype(F32)).astype(BF16)
            b = _dot(tri[d], hi) + _dot(tri[d], lo)
            edge = CHUNK - 1 if d == 0 else 0
            tot = jnp.concatenate(
                [jnp.broadcast_to(b[cc * CHUNK + edge:cc * CHUNK + edge + 1, :], (CHUNK, GLA_DK_W))
                 for cc in range(BLK // CHUNK)], axis=0)
            qt_s[d, rows, :] = (q * jnp.exp(b)).astype(BF16)
            kt_s[d, rows, :] = (k * jnp.exp(-b)).astype(BF16)
            ke_s[d, rows, :] = (k * jnp.exp(tot - b)).astype(BF16)
            for cc in range(BLK // CHUNK):
                c = blk * (BLK // CHUNK) + cc
                dec_s[d, c:c + 1, :] = jnp.exp(tot[cc * CHUNK:cc * CHUNK + 1, :])
    for d in range(2):
        if has_s0:
            st_s[d] = jnp.concatenate([s0_ref[d, h] for h in range(GLA_HEADS)], axis=0).T
        else:
            st_s[d] = jnp.zeros((DV, GLA_DK_W), F32)

    lane_head = lax.broadcasted_iota(jnp.int32, (1, GLA_DK_W), 1) // DK
    head_mask = [jnp.where(lane_head == h, 1.0, 0.0).astype(BF16) for h in range(GLA_HEADS)]
    ti = lax.broadcasted_iota(jnp.int32, (CHUNK, CHUNK), 0)
    tj = lax.broadcasted_iota(jnp.int32, (CHUNK, CHUNK), 1)
    causal = (tj <= ti, tj >= ti)

    def pair_step(chunks):
        rows = []
        for c in chunks:
            r0 = c * CHUNK
            rows.append(pl.ds(r0 if isinstance(r0, int) else pl.multiple_of(r0, CHUNK), CHUNK))
        v_c = [v_ref[rows[d], :] for d in range(2)]
        res = []
        for d in range(2):
            qt_c = qt_s[d, rows[d], :]
            q_stack = jnp.concatenate([qt_c * head_mask[h] for h in range(GLA_HEADS)], axis=0)
            w = jnp.concatenate([st_s[d].astype(BF16), kt_s[d, rows[d], :]], axis=0)
            res.append(_dot_nt(q_stack, w))
        kv = []
        for d in range(2):
            ke_c = ke_s[d, rows[d], :]
            k_bd = jnp.concatenate([ke_c * head_mask[h] for h in range(GLA_HEADS)], axis=0)
            v_stack = jnp.concatenate(
                [v_c[d][:, h * DV:(h + 1) * DV] for h in range(GLA_HEADS)], axis=0)
            v_stack_t = v_stack.astype(F32).T.astype(BF16)
            kv.append(_dot(v_stack_t, k_bd))
        for d in range(2):
            c = chunks[d]
            st_s[d] = dec_s[d, pl.ds(c, 1), :] * st_s[d] + kv[d]
        for d in range(2):
            for h in range(GLA_HEADS):
                hr = slice(h * CHUNK, (h + 1) * CHUNK)
                hv = slice(h * DV, (h + 1) * DV)
                att = jnp.where(causal[d], res[d][hr, DV:DV + CHUNK], 0.0).astype(BF16)
                o_s[d, rows[d], hv] = res[d][hr, 0:DV] + _dot(att, v_c[d][:, hv])

    if nc <= 4:
        for i in range(nc):
            pair_step((i, nc - 1 - i))
    else:
        def body(i, carry):
            pair_step((i, nc - 1 - i))
            return carry
        lax.fori_loop(0, nc, body, 0, unroll=4)

    g = g_ref[...]
    for h in range(GLA_HEADS):
        hv = slice(h * DV, (h + 1) * DV)
        o = o_s[0, :, hv] + o_s[1, :, hv]
        og_ref[:, hv] = (_rms(o, g) * sg_ref[:, hv].astype(F32)).astype(BF16)
    if want_state:
        for d in range(2):
            s_d = st_s[d].T
            for h in range(GLA_HEADS):
                sT_ref[d, h] = s_d[h * DK:(h + 1) * DK, :]


def _gla(q, k, v, laf, lab, sg, g_gla, state_in, state_out, layer, batch, seq_len):
    has_s0 = state_in is not None
    want_state = state_out is not None
    r3 =lambda a: a.reshape(batch, seq_len, a.shape[-1])
    seq = lambda w: pl.BlockSpec((None, seq_len, w), lambda b: (b, 0, 0))
    st_spec = pl.BlockSpec((None, None, 2, GLA_HEADS, DK, DV), lambda b: (b, layer, 0, 0, 0, 0))
    in_specs = []
    args = []
    if has_s0:
        in_specs.append(st_spec)
        args.append(state_in)
    if want_state:
        in_specs.append(pl.BlockSpec(memory_space=pl.ANY))
        args.append(state_out)
    in_specs += [seq(GLA_DK_W), seq(GLA_DK_W), seq(GLA_DV_W), seq(GLA_DK_W), seq(GLA_DK_W),
                 seq(GLA_DV_W), pl.BlockSpec((1, DV), lambda b: (0, 0))]
    args += [r3(q), r3(k), r3(v), r3(laf), r3(lab), r3(sg), g_gla]
    out_specs = [seq(GLA_DV_W)]
    out_shape = [jax.ShapeDtypeStruct((batch, seq_len, GLA_DV_W), BF16)]
    if want_state:
        out_specs.append(st_spec)
        out_shape.append(jax.ShapeDtypeStruct((batch, DEPTH, 2, GLA_HEADS, DK, DV), F32))
    res = pl.pallas_call(
        functools.partial(_gla_kernel, seq_len=seq_len, has_s0=has_s0, want_state=want_state),
        grid=(batch,),
        in_specs=in_specs,
        out_specs=out_specs,
        out_shape=out_shape,
        input_output_aliases={int(has_s0): 1} if want_state else {},
        scratch_shapes=[
            pltpu.VMEM((2, seq_len, GLA_DK_W), BF16),
            pltpu.VMEM((2, seq_len, GLA_DK_W), BF16),
            pltpu.VMEM((2, seq_len, GLA_DK_W), BF16),
            pltpu.VMEM((2, seq_len // CHUNK, GLA_DK_W), F32),
            pltpu.VMEM((2, seq_len, GLA_DV_W), F32),
            pltpu.VMEM((2, DV, GLA_DK_W), F32),
        ],
        compiler_params=pltpu.CompilerParams(
            dimension_semantics=("arbitrary",), vmem_limit_bytes=VMEM_LIMIT),
        name="gla",
    )(*args)
    return res[0], (res[1] if want_state else None)


FF_CHUNK = 256


def _tail_kernel(*refs, has_pe, seg, tm):
    if has_pe:
        x_ref, pr_ref, pc_ref, *refs = refs
    else:
        x_ref, *refs = refs
    (yf_ref, og_ref, mod_ref, wo_ref, gpm_ref, gpf_ref, gqf_ref,
     wup_ref, cw_ref, cb_ref, wdn_ref, out_ref, x1_s, a_s) = refs
    mod = mod_ref[0]
    x = x_ref[...]
    if has_pe:
        x = _add_pos(x, pr_ref, pc_ref)
    y = _dot(yf_ref[...], wo_ref[0:FOURIER_W, :]) + _dot(og_ref[...], wo_ref[FOURIER_W:, :])
    gate_m = mod[:, 2 * D_MODEL:3 * D_MODEL]
    x1 = x + gate_m * _rms(y, gpm_ref[...])
    x1_s[...] = x1
    shift_f = mod[:, 3 * D_MODEL:4 * D_MODEL]
    scale_f = mod[:, 4 * D_MODEL:5 * D_MODEL]
    h = (_rms(x1, gpf_ref[...]) * (1.0 + scale_f) + shift_f).astype(BF16)

    pos = lax.broadcasted_iota(jnp.int32, (tm, 1), 0) % seg
    first = pos == 0
    last = pos == seg - 1

    def conv_cols(c0):
        cols = slice(c0, c0 + FF_CHUNK)
        u = _dot(h, wup_ref[:, cols])
        u_prev = jnp.where(first, 0.0, pltpu.roll(u, 1, 0))
        u_next = jnp.where(last, 0.0, pltpu.roll(u, tm - 1, 0))
        return (u_prev * cw_ref[0:1, cols] + u * cw_ref[1:2, cols] + u_next * cw_ref[2:3, cols]
                + cb_ref[:, cols])

    for c in range(D_FF // FF_CHUNK):
        val = conv_cols(c * FF_CHUNK)
        gate = conv_cols(D_FF + c * FF_CHUNK)
        a_s[:, c * FF_CHUNK:(c + 1) * FF_CHUNK] = (_silu(gate) * val).astype(BF16)

    y2 = _dot(a_s[...], wdn_ref[...])
    gate_f = mod[:, 5 * D_MODEL:6 * D_MODEL]
    out_ref[...] = x1_s[...] + gate_f * _rms(y2, gqf_ref[...])


def _tail(x, pe, yf, og, mod, w_out, gpm, gpf, gqf, w_up, cw, cb, w_down, seq_len, seg, tm):
    n = x.shape[0]
    has_pe = pe is not None
    per_batch = mod.shape[0] > 1
    mod_idx = (lambda i: ((i * tm) // seq_len, 0, 0)) if per_batch else (lambda i: (0, 0, 0))
    row = lambda w: pl.BlockSpec((tm, w), lambda i: (i, 0))
    full = lambda a: pl.BlockSpec(a.shape, lambda i: (0,) * a.ndim, pipeline_mode=pl.Buffered(1))
    in_specs = [row(D_MODEL)]
    args = [x]
    if has_pe:
        in_specs += _pos_specs(pe, seq_len, tm)
        args += list(pe)
    in_specs += [
        row(FOURIER_W), row(GLA_DV_W), pl.BlockSpec((1, 1, 6 * D_MODEL), mod_idx),
        full(w_out), full(gpm), full(gpf), full(gqf), full(w_up), full(cw), full(cb), full(w_down),
    ]
    args += [yf, og, mod, w_out, gpm, gpf, gqf, w_up, cw, cb, w_down]
    return pl.pallas_call(
        functools.partial(_tail_kernel, has_pe=has_pe, seg=seg, tm=tm),
        grid=(n // tm,),
        in_specs=in_specs,
        out_specs=row(D_MODEL),
        out_shape=jax.ShapeDtypeStruct((n, D_MODEL), F32),
        scratch_shapes=[
            pltpu.VMEM((tm, D_MODEL), F32),
            pltpu.VMEM((tm, D_FF), BF16),
        ],
        compiler_params=pltpu.CompilerParams(
            dimension_semantics=("arbitrary",), vmem_limit_bytes=VMEM_LIMIT),
        name="tail",
    )(*args)


def _grid_pos_embed(rows, d):
    quarter = d // 4
    omega = 1.0 / (POS_BASE ** (jnp.arange(quarter, dtype=F32) / quarter))
    er = jnp.arange(rows, dtype=F32)[:, None] * omega
    ec = jnp.arange(GRID_W, dtype=F32)[:, None] * omega
    pr = jnp.concatenate([jnp.sin(er), jnp.cos(er)], axis=-1)
    pc = jnp.concatenate([jnp.sin(ec), jnp.cos(ec)], axis=-1)
    return pr, pc


def _dft_tables(n, scale):
    idx = jnp.arange(n, dtype=jnp.int32)
    ang = ((idx[:, None] * idx[None, :]) % n).astype(F32) * np.float32(2.0 * np.pi / n)
    return jnp.cos(ang) * np.float32(scale), jnp.sin(ang) * np.float32(scale)


def _seqdft_tables(seq_len):
    p = _dft_len(seq_len)
    radix = seq_len // p
    c, s = _dft_tables(p, seq_len ** -0.5)
    tables = (c.astype(BF16), (-s).astype(BF16))
    if radix > 1:
        k = (jnp.arange(p, dtype=jnp.int32)[:, None, None]
             + p * jnp.arange(radix, dtype=jnp.int32)[None, :, None])
        r = jnp.arange(radix, dtype=jnp.int32)[None, None, :]
        ang = ((k * r) % seq_len).astype(F32) * np.float32(2.0 * np.pi / seq_len)
        ang = ang.reshape(p, radix * radix)
        tables += (jnp.cos(ang), jnp.sin(ang))
    return tables


def _layer_weights(l, w_in, w_gate_f, b_gate_f, w_gate_b, b_gate_b, w_out, w_up, conv_w, conv_b,
                   w_down):
    wi = w_in[l]
    w_in_r = jnp.concatenate(
        [wi[:, :S_V], wi[:, S_AB:], wi[:, S_V:S_AB],
         jnp.zeros((D_MODEL, LANE - 2 * GATE_RANK), F32)], axis=1).astype(BF16)
    wg = jnp.zeros((LANE, 2 * GLA_DK_W), F32)
    wg = wg.at[0:GATE_RANK, 0:GLA_DK_W].set(w_gate_f[l])
    wg = wg.at[GATE_RANK:2 * GATE_RANK, GLA_DK_W:].set(w_gate_b[l])
    bg = jnp.concatenate([b_gate_f[l], b_gate_b[l]])[None]

    return dict(
        w_in_r=w_in_r, wg=wg.astype(BF16), bg=bg, w_out=w_out[l].astype(BF16),
        w_up=w_up[l].astype(BF16), cw=conv_w[l], cb=conv_b[l][None],
        w_down=w_down[l].astype(BF16))


def _stream_layer(x, pe, mod, state_in, state_out, layer, lw, gains, dft, batch, seq_len, seg,
                  tm_in, nb_dft, tm_tail):
    g_pre_mix, g_post_mix, g_pre_ffn, g_post_ffn, g_gla = gains
    cc, seq_tables = dft
    ab, q, k, v, laf, lab, sg = _inproj(x, pe, mod, g_pre_mix, lw['w_in_r'], cc, lw['wg'],
                                        lw['bg'], seq_len, tm_in)
    yf = _seqdft(ab, seq_tables, batch, seq_len, nb_dft)
    og, s_t = _gla(q, k, v, laf, lab, sg, g_gla, state_in, state_out, layer, batch, seq_len)
    x_new = _tail(x, pe, yf, og.reshape(batch * seq_len, GLA_DV_W), mod, lw['w_out'], g_post_mix,
                  g_pre_ffn, g_post_ffn, lw['w_up'], lw['cw'], lw['cb'], lw['w_down'],
                  seq_len, seg, tm_tail)
    return x_new, s_t


def kernel(x_prompt, x_sample, state_gla, c, c_ctx, g_pre_mix, g_post_mix, g_pre_ffn, g_post_ffn,
           w_ada, b_ada, w_in, w_gate_f, b_gate_f, w_gate_b, b_gate_b, g_gla, w_out, w_up, conv_w,
           conv_b, w_down):
    pb, pt, _ = x_prompt.shape
    sb, s_t, _ = x_sample.shape
    rows = s_t // GRID_W

    cv = jnp.concatenate([c_ctx[None], c, jnp.zeros((MOD_ROWS - 1 - sb, D_MODEL), F32)], axis=0)
    mod_all = _ada(cv, w_ada, b_ada)

    cc_c, cc_s = _dft_tables(FOURIER_HD, FOURIER_HD ** -0.5)
    cc = jnp.concatenate([cc_c, cc_s], axis=1).astype(BF16)
    dft_p = (cc, _seqdft_tables(pt))
    dft_s = (cc, _seqdft_tables(s_t))
    pe = _grid_pos_embed(rows, D_MODEL)

    xp = x_prompt.reshape(pb * pt, D_MODEL)
    xs = x_sample.reshape(sb * s_t, D_MODEL)
    new_state = jnp.zeros((pb, DEPTH, 2, GLA_HEADS, DK, DV), F32)
    for l in range(DEPTH):
        lw = _layer_weights(l, w_in, w_gate_f, b_gate_f, w_gate_b, b_gate_b, w_out, w_up, conv_w,
                            conv_b, w_down)
        gains = (g_pre_mix[l][None], g_post_mix[l][None], g_pre_ffn[l][None],
                 g_post_ffn[l][None], g_gla[l][None])
        mod_p = mod_all[l, 0:1][:, None]
        mod_s = mod_all[l, 1:1 + sb][:, None]
        xp, new_state = _stream_layer(xp, None, mod_p, None, new_state, l, lw, gains, dft_p,
                                      pb, pt, pt, 512, 4, 512)
        xs, _ = _stream_layer(xs, pe if l == 0 else None, mod_s, state_gla, None, l, lw, gains,
                              dft_s, sb, s_t, GRID_W, 512, 1, 512)
    return (xp.reshape(pb, pt, D_MODEL), xs.reshape(sb, s_t, D_MODEL), new_state)
```

```python
import functools

import numpy as np
import jax
import jax.numpy as jnp
from jax import lax
from jax.experimental import pallas as pl
from jax.experimental.pallas import tpu as pltpu

D_MODEL = 1024
DEPTH = 2
GRID_W = 64
FOURIER_W = 512
FOURIER_HEADS = 4
FOURIER_HD = FOURIER_W // FOURIER_HEADS
GLA_DV_W = 512
GLA_DK_W = 256
GLA_HEADS = 4
DV = GLA_DV_W // GLA_HEADS
DK = GLA_DK_W // GLA_HEADS
GATE_RANK = 16
GATE_TEMP = 16.0
CHUNK = 64
D_FF = 11 * D_MODEL // 4
EPS = 1e-6
POS_BASE = 10000.0
S_F = FOURIER_W
S_Q = S_F + GLA_DK_W
S_K = S_Q + GLA_DK_W
S_V = S_K + GLA_DV_W
S_AF = S_V + GATE_RANK
S_AB = S_AF + GATE_RANK
IN_COLS = S_AB + GLA_DV_W

C_F = 0
C_Q = C_F + FOURIER_W
C_K = C_Q + GLA_DK_W
C_V = C_K + GLA_DK_W
C_G = C_V + GLA_DV_W
C_A = C_G + GLA_DV_W
LANE = 128
IN_COLS_PAD = C_A + LANE
MOD_ROWS = 8
BLK = 256

VMEM_LIMIT = 56 * 1024 * 1024

F32 = jnp.float32
BF16 = jnp.bfloat16


def _silu(x):
    return x * jax.nn.sigmoid(x)


def _rms(x, g):
    return x * lax.rsqrt(jnp.mean(x * x, axis=-1, keepdims=True) + EPS) * g


def _dot(a, b):
    return jnp.dot(a, b, preferred_element_type=F32)


def _dot_nt(a, b):
    return lax.dot_general(a, b, (((1,), (1,)), ((), ())), preferred_element_type=F32)


def _add_pos(x, pr_ref, pc_ref):
    half = D_MODEL // 2
    reps = x.shape[0] // GRID_W
    row_part = jnp.concatenate(
        [jnp.broadcast_to(pr_ref[j:j + 1, :], (GRID_W, half)) for j in range(reps)], axis=0)
    col_part = jnp.concatenate([pc_ref[...]] * reps, axis=0)
    return x + jnp.concatenate([row_part, col_part], axis=1)


def _ada_kernel(cv_ref, w_ref, b_ref, o_ref):
    s = _silu(cv_ref[...]).astype(BF16)
    o_ref[0] = _dot(s, w_ref[0].astype(BF16)) + b_ref[0]


def _ada(cv, w_ada, b_ada):
    tn = 1536
    n_out = w_ada.shape[-1]
    return pl.pallas_call(
        _ada_kernel,
        grid=(DEPTH, n_out // tn),
        in_specs=[
            pl.BlockSpec((MOD_ROWS, D_MODEL), lambda l, n: (0, 0)),
            pl.BlockSpec((1, D_MODEL, tn), lambda l, n: (l, 0, n)),
            pl.BlockSpec((1, 1, tn), lambda l, n: (l, 0, n)),
        ],
        out_specs=pl.BlockSpec((1, MOD_ROWS, tn), lambda l, n: (l, 0, n)),
        out_shape=jax.ShapeDtypeStruct((DEPTH, MOD_ROWS, n_out), F32),
        compiler_params=pltpu.CompilerParams(
            dimension_semantics=("arbitrary", "arbitrary"), vmem_limit_bytes=VMEM_LIMIT),
        name="ada",
    )(cv, w_ada, b_ada.reshape(DEPTH, 1, n_out))


def _mod_row(tile, tm, seq_len, per_batch):
    return 1 + (tile * tm) // seq_len if per_batch else 0


def _inproj_kernel(*refs, has_pe, per_batch, radix, tm, seq_len):
    if has_pe:
        x_ref, pr_ref, pc_ref, *refs = refs
    else:
        x_ref, *refs = refs
    if radix > 1:
        *refs, ab_s = refs
    (mod_ref, g_ref, w_ref, wgf_ref, wgb_ref, bgf_ref, bgb_ref, cc_ref,
     ab_ref, q_ref, k_ref, v_ref, laf_ref, lab_ref, sg_ref, w_s, wg_s) = refs
    i = pl.program_id(0)

    def cast_weights():
        rows_per = 256
        for r0 in range(0, D_MODEL, rows_per):
            rs = slice(r0, r0 + rows_per)
            w_s[rs, C_F:C_G] = w_ref[rs, 0:S_V].astype(BF16)
            w_s[rs, C_G:C_A] = w_ref[rs, S_AB:IN_COLS].astype(BF16)
            w_s[rs, C_A:IN_COLS_PAD] = jnp.zeros((rows_per, LANE), BF16)
            w_s[rs, C_A:C_A + 2 * GATE_RANK] = w_ref[rs, S_V:S_AB].astype(BF16)
        wg_s[...] = jnp.zeros_like(wg_s)
        wg_s[0:GATE_RANK, 0:GLA_DK_W] = wgf_ref[...].astype(BF16)
        wg_s[GATE_RANK:2 * GATE_RANK, GLA_DK_W:] = wgb_ref[...].astype(BF16)

    def norm_tile():
        x = x_ref[...]
        if has_pe:
            x = _add_pos(x, pr_ref, pc_ref)
        mod = mod_ref[pl.ds(_mod_row(i, tm, seq_len, per_batch), 1), :]
        shift = mod[:, 0:D_MODEL]
        gain = g_ref[...] * (1.0 + mod[:, D_MODEL:2 * D_MODEL])
        return (_rms(x, gain) + shift).astype(BF16)

    def project_tile(h):
        z = _dot(h, w_s[...])
        zf = z[:, C_F:C_Q].astype(BF16)
        for hh in range(FOURIER_HEADS):
            sl = slice(hh * FOURIER_HD, (hh + 1) * FOURIER_HD)
            cs = _dot(zf[:, sl], cc_ref[...])
            if radix == 1:
                ab_ref[0, :, sl] = cs[:, :FOURIER_HD].astype(BF16)
                ab_ref[1, :, sl] = cs[:, FOURIER_HD:].astype(BF16)
            else:
                ab_s[0, hh] = cs[:, :FOURIER_HD]
                ab_s[1, hh] = cs[:, FOURIER_HD:]
        if radix > 1:
            for r in range(radix):
                for part in range(2):
                    for hh in range(FOURIER_HEADS):
                        c0 = r * FOURIER_W + hh * FOURIER_HD
                        ab_ref[part, :, c0:c0 + FOURIER_HD] = (
                            ab_s[part, hh, pl.ds(r, tm // radix, stride=radix), :].astype(BF16))
        q_ref[...] = z[:, C_Q:C_K] * (DK ** -0.5)
        k_ref[...] = z[:, C_K:C_V]
        v_ref[...] = z[:, C_V:C_G].astype(BF16)
        sg_ref[...] = _silu(z[:, C_G:C_A]).astype(BF16)
        xg = _dot(z[:, C_A:IN_COLS_PAD].astype(BF16), wg_s[...])
        for ref, bias, cols in ((laf_ref, bgf_ref, slice(0, GLA_DK_W)),
                                (lab_ref, bgb_ref, slice(GLA_DK_W, 2 * GLA_DK_W))):
            xb = xg[:, cols] + bias[...]
            ref[...] = (jnp.minimum(xb, 0.0) - jnp.log1p(jnp.exp(-jnp.abs(xb)))) * (1.0 / GATE_TEMP)

    pl.when(i == 0)(cast_weights)
    project_tile(norm_tile())


def _inproj(x, pe, mod_all, per_batch, layer, g, w_in, w_gate_f, b_gate_f, w_gate_b, b_gate_b, cc,
            seq_len, tm):
    n = x.shape[0]
    n_tiles = n // tm
    radix = seq_len // _dft_len(seq_len)
    has_pe = pe is not None
    once = pl.Buffered(1)
    lay = lambda a: pl.BlockSpec((None,) + a.shape[1:], lambda i: (layer,) + (0,) * (a.ndim - 1),
                                 pipeline_mode=once)
    row = lambda w: pl.BlockSpec((tm, w), lambda i: (i, 0))
    in_specs = [row(D_MODEL)]
    args = [x]
    if has_pe:
        pr, pc = pe
        in_specs += [pl.BlockSpec((tm // GRID_W, pr.shape[1]), lambda i: (i % (seq_len // tm), 0)),
                     pl.BlockSpec(pc.shape, lambda i: (0, 0), pipeline_mode=once)]
        args += [pr, pc]
    weights = [mod_all, g, w_in, w_gate_f, w_gate_b, b_gate_f, b_gate_b]
    in_specs += [lay(a) for a in weights] + [pl.BlockSpec(cc.shape, lambda i: (0, 0),
                                                          pipeline_mode=once)]
    args += weights + [cc]
    out_shape = (
        jax.ShapeDtypeStruct((2, n // radix, radix * FOURIER_W), BF16),
        jax.ShapeDtypeStruct((n, GLA_DK_W), F32),
        jax.ShapeDtypeStruct((n, GLA_DK_W), F32),
        jax.ShapeDtypeStruct((n, GLA_DV_W), BF16),
        jax.ShapeDtypeStruct((n, GLA_DK_W), F32),
        jax.ShapeDtypeStruct((n, GLA_DK_W), F32),
        jax.ShapeDtypeStruct((n, GLA_DV_W), BF16),
    )
    out_specs = (
        pl.BlockSpec((2, tm // radix, radix * FOURIER_W), lambda i: (0, i, 0)),
        row(GLA_DK_W), row(GLA_DK_W), row(GLA_DV_W), row(GLA_DK_W), row(GLA_DK_W), row(GLA_DV_W),
    )
    scratch = [pltpu.VMEM((D_MODEL, IN_COLS_PAD), BF16), pltpu.VMEM((LANE, 2 * GLA_DK_W), BF16)]
    if radix > 1:
        scratch.append(pltpu.VMEM((2, FOURIER_HEADS, tm, FOURIER_HD), F32))
    return pl.pallas_call(
        functools.partial(_inproj_kernel, has_pe=has_pe, per_batch=per_batch, radix=radix, tm=tm,
                          seq_len=seq_len),
        grid=(n_tiles,),
        in_specs=in_specs,
        out_specs=out_specs,
        out_shape=out_shape,
        scratch_shapes=scratch,
        compiler_params=pltpu.CompilerParams(
            dimension_semantics=("arbitrary",), vmem_limit_bytes=VMEM_LIMIT),
        name="inproj",
    )(*args)


DFT_MAX = 512


def _dft_len(seq_len):
    return min(seq_len, DFT_MAX)


def _seqdft_kernel(*refs, radix, nb):
    if radix > 1:
        c_ref, sn_ref, twc_ref, tws_ref, a_ref, b_ref, o_ref = refs
    else:
        c_ref, sn_ref, a_ref, b_ref, o_ref = refs
    c = c_ref[...]
    sn = sn_ref[...]
    for bi in range(nb):
        a = a_ref[bi]
        b = b_ref[bi]
        g_re = _dot(c, a) + _dot(sn, b)
        if radix == 1:
            o_ref[bi, 0] = g_re.astype(BF16)
            continue
        g_im = _dot(sn, a) - _dot(c, b)
        for k2 in range(radix):
            acc = g_re[:, 0:FOURIER_W]
            for r in range(1, radix):
                j = k2 * radix + r
                cols = slice(r * FOURIER_W, (r + 1) * FOURIER_W)
                acc = acc + twc_ref[:, j:j + 1] * g_re[:, cols] + tws_ref[:, j:j + 1] * g_im[:, cols]
            o_ref[bi, k2] = acc.astype(BF16)


def _seqdft(ab, tables, batch, seq_len, nb):
    p = _dft_len(seq_len)
    radix = seq_len // p
    width = radix * FOURIER_W
    ab4 = ab.reshape(2, batch, p, width)
    full = lambda a: pl.BlockSpec(a.shape, lambda i: (0,) * a.ndim)
    out = pl.pallas_call(
        functools.partial(_seqdft_kernel, radix=radix, nb=nb),
        grid=(batch // nb,),
        in_specs=[full(t) for t in tables] + [
            pl.BlockSpec((None, nb, p, width), lambda i: (0, i, 0, 0)),
            pl.BlockSpec((None, nb, p, width), lambda i: (1, i, 0, 0)),
        ],
        out_specs=pl.BlockSpec((nb, radix, p, FOURIER_W), lambda i: (i, 0, 0, 0)),
        out_shape=jax.ShapeDtypeStruct((batch, radix, p, FOURIER_W), BF16),
        compiler_params=pltpu.CompilerParams(
            dimension_semantics=("arbitrary",), vmem_limit_bytes=VMEM_LIMIT),
        name="seqdft",
    )(*tables, ab4, ab4)
    return out.reshape(batch * seq_len, FOURIER_W)


def _gla_kernel(*refs, seq_len, has_s0, want_state):
    if has_s0:
        s0_ref, *refs = refs
    if want_state:
        _, *refs = refs
    q_ref, k_ref, v_ref, laf_ref, lab_ref, sg_ref, g_ref, og_ref, *refs = refs
    if want_state:
        sT_ref, *refs = refs
    qt_s, kt_s, ke_s, dec_s, o_s, st_s = refs
    nc = seq_len // CHUNK
    la_refs = (laf_ref, lab_ref)

    ri = lax.broadcasted_iota(jnp.int32, (BLK, BLK), 0)
    ci = lax.broadcasted_iota(jnp.int32, (BLK, BLK), 1)
    same = (ri // CHUNK) == (ci // CHUNK)
    tri = (jnp.where(same & (ci <= ri), 1.0, 0.0).astype(BF16),
           jnp.where(same & (ci >= ri), 1.0, 0.0).astype(BF16))

    for blk in range(seq_len // BLK):
        rows = slice(blk * BLK, (blk + 1) * BLK)
        q = q_ref[rows, :]
        k = k_ref[rows, :]
        for d in range(2):
            la = la_refs[d][rows, :]
            hi = la.astype(BF16)
            lo = (la - hi.astype(F32)).astype(BF16)
            b = _dot(tri[d], hi) + _dot(tri[d], lo)
            edge = CHUNK - 1 if d == 0 else 0
            tot = jnp.concatenate(
                [jnp.broadcast_to(b[cc * CHUNK + edge:cc * CHUNK + edge + 1, :], (CHUNK, GLA_DK_W))
                 for cc in range(BLK // CHUNK)], axis=0)
            qt_s[d, rows, :] = (q * jnp.exp(b)).astype(BF16)
            kt_s[d, rows, :] = (k * jnp.exp(-b)).astype(BF16)
            ke_s[d, rows, :] = (k * jnp.exp(tot - b)).astype(BF16)
            for cc in range(BLK // CHUNK):
                c = blk * (BLK // CHUNK) + cc
                dec_s[d, c:c + 1, :] = jnp.exp(tot[cc * CHUNK:cc * CHUNK + 1, :])
    for d in range(2):
        if has_s0:
            st_s[d] = jnp.concatenate([s0_ref[d, h] for h in range(GLA_HEADS)], axis=0).T
        else:
            st_s[d] = jnp.zeros((DV, GLA_DK_W), F32)

    lane_head = lax.broadcasted_iota(jnp.int32, (1, GLA_DK_W), 1) // DK
    head_mask = [jnp.where(lane_head == h, 1.0, 0.0).astype(BF16) for h in range(GLA_HEADS)]
    ti = lax.broadcasted_iota(jnp.int32, (CHUNK, CHUNK), 0)
    tj = lax.broadcasted_iota(jnp.int32, (CHUNK, CHUNK), 1)
    causal = (tj <= ti, tj >= ti)

    def pair_step(chunks):
        rows = []
        for c in chunks:
            r0 = c * CHUNK
            rows.append(pl.ds(r0 if isinstance(r0, int) else pl.multiple_of(r0, CHUNK), CHUNK))
        v_c = [v_ref[rows[d], :] for d in range(2)]
        res = []
        for d in range(2):
            qt_c = qt_s[d, rows[d], :]
            q_stack = jnp.concatenate([qt_c * head_mask[h] for h in range(GLA_HEADS)], axis=0)
            w = jnp.concatenate([st_s[d].astype(BF16), kt_s[d, rows[d], :]], axis=0)
            res.append(_dot_nt(q_stack, w))
        kv = []
        for d in range(2):
            ke_c = ke_s[d, rows[d], :]
            k_bd = jnp.concatenate([ke_c * head_mask[h] for h in range(GLA_HEADS)], axis=0)
            v_stack = jnp.concatenate(
                [v_c[d][:, h * DV:(h + 1) * DV] for h in range(GLA_HEADS)], axis=0)
            v_stack_t = v_stack.astype(F32).T.astype(BF16)
            kv.append(_dot(v_stack_t, k_bd))
        for d in range(2):
            c = chunks[d]
            st_s[d] = dec_s[d, pl.ds(c, 1), :] * st_s[d] + kv[d]
        for d in range(2):
            for h in range(GLA_HEADS):
                hr = slice(h * CHUNK, (h + 1) * CHUNK)
                hv = slice(h * DV, (h + 1) * DV)
                att = jnp.where(causal[d], res[d][hr, DV:DV + CHUNK], 0.0).astype(BF16)
                o_s[d, rows[d], hv] = res[d][hr, 0:DV] + _dot(att, v_c[d][:, hv])

    if nc <= 4:
        for i in range(nc):
            pair_step((i, nc - 1 - i))
    else:
        def body(i, carry):
            pair_step((i, nc - 1 - i))
            return carry
        lax.fori_loop(0, nc, body, 0, unroll=4)

    g = g_ref[...]
    for h in range(GLA_HEADS):
        hv = slice(h * DV, (h + 1) * DV)
        o = o_s[0, :, hv] + o_s[1, :, hv]
        og_ref[:, hv] = (_rms(o, g) * sg_ref[:, hv].astype(F32)).astype(BF16)
    if want_state:
        for d in range(2):
            s_d = st_s[d].T
            for h in range(GLA_HEADS):
                sT_ref[d, h] = s_d[h * DK:(h + 1) * DK, :]


def _gla(q, k, v, laf, lab, sg, g_gla, state_in, state_out, layer, batch, seq_len):
    has_s0 = state_in is not None
    want_state = state_out is not None
    r3 =lambda a: a.reshape(batch, seq_len, a.shape[-1])
    seq = lambda w: pl.BlockSpec((None, seq_len, w), lambda b: (b, 0, 0))
    st_spec = pl.BlockSpec((None, None, 2, GLA_HEADS, DK, DV), lambda b: (b, layer, 0, 0, 0, 0))
    in_specs = []
    args = []
    if has_s0:
        in_specs.append(st_spec)
        args.append(state_in)
    if want_state:
        in_specs.append(pl.BlockSpec(memory_space=pl.ANY))
        args.append(state_out)
    in_specs += [seq(GLA_DK_W), seq(GLA_DK_W), seq(GLA_DV_W), seq(GLA_DK_W), seq(GLA_DK_W),
                 seq(GLA_DV_W), pl.BlockSpec((None, 1, DV), lambda b: (layer, 0, 0))]
    args += [r3(q), r3(k), r3(v), r3(laf), r3(lab), r3(sg), g_gla]
    out_specs = [seq(GLA_DV_W)]
    out_shape = [jax.ShapeDtypeStruct((batch, seq_len, GLA_DV_W), BF16)]
    if want_state:
        out_specs.append(st_spec)
        out_shape.append(jax.ShapeDtypeStruct((batch, DEPTH, 2, GLA_HEADS, DK, DV), F32))
    res = pl.pallas_call(
        functools.partial(_gla_kernel, seq_len=seq_len, has_s0=has_s0, want_state=want_state),
        grid=(batch,),
        in_specs=in_specs,
        out_specs=out_specs,
        out_shape=out_shape,
        input_output_aliases={int(has_s0): 1} if want_state else {},
        scratch_shapes=[
            pltpu.VMEM((2, seq_len, GLA_DK_W), BF16),
            pltpu.VMEM((2, seq_len, GLA_DK_W), BF16),
            pltpu.VMEM((2, seq_len, GLA_DK_W), BF16),
            pltpu.VMEM((2, seq_len // CHUNK, GLA_DK_W), F32),
            pltpu.VMEM((2, seq_len, GLA_DV_W), F32),
            pltpu.VMEM((2, DV, GLA_DK_W), F32),
        ],
        compiler_params=pltpu.CompilerParams(
            dimension_semantics=("arbitrary",), vmem_limit_bytes=VMEM_LIMIT),
        name="gla",
    )(*args)
    return res[0], (res[1] if want_state else None)


FF_CHUNK = 256


def _tail_kernel(*refs, has_pe, per_batch, seg, tm, seq_len):
    if has_pe:
        x_ref, pr_ref, pc_ref, *refs = refs
    else:
        x_ref, *refs = refs
    (yf_ref, og_ref, mod_ref, wo_ref, gpm_ref, gpf_ref, gqf_ref,
     wup_ref, cw_ref, cb_ref, wdn_ref, out_ref, x1_s, a_s) = refs
    i = pl.program_id(0)
    mod = mod_ref[pl.ds(_mod_row(i, tm, seq_len, per_batch), 1), :]

    x = x_ref[...]
    if has_pe:
        x = _add_pos(x, pr_ref, pc_ref)
    y = _dot(yf_ref[...], wo_ref[0:FOURIER_W, :]) + _dot(og_ref[...], wo_ref[FOURIER_W:, :])
    x1 = x + _rms(y, gpm_ref[...] * mod[:, 2 * D_MODEL:3 * D_MODEL])
    x1_s[...] = x1
    shift_f = mod[:, 3 * D_MODEL:4 * D_MODEL]
    gain_f = gpf_ref[...] * (1.0 + mod[:, 4 * D_MODEL:5 * D_MODEL])
    h = (_rms(x1, gain_f) + shift_f).astype(BF16)

    pos = lax.broadcasted_iota(jnp.int32, (tm, 1), 0) % seg
    first = pos == 0
    last = pos == seg - 1

    def conv_cols(c0, width):
        cols = slice(c0, c0 + width)
        u = _dot(h, wup_ref[:, cols])
        u_prev = jnp.where(first, 0.0, pltpu.roll(u, 1, 0))
        u_next = jnp.where(last, 0.0, pltpu.roll(u, tm - 1, 0))
        return (u_prev * cw_ref[0:1, cols] + u * cw_ref[1:2, cols]
                + u_next * cw_ref[2:3, cols] + cb_ref[:, cols])

    for c0 in range(0, D_FF, FF_CHUNK):
        width = min(FF_CHUNK, D_FF - c0)
        val = conv_cols(c0, width)
        gate = conv_cols(D_FF + c0, width)
        a_s[:, c0:c0 + width] = (_silu(gate) * val).astype(BF16)

    y2 = _dot(a_s[...], wdn_ref[...])
    out_ref[...] = x1_s[...] + _rms(y2, gqf_ref[...] * mod[:, 5 * D_MODEL:6 * D_MODEL])


def _tail(x, pe, yf, og, mod_all, per_batch, layer, w_out, gpm, gpf, gqf, w_up, cw, cb, w_down,
          seq_len, seg, tm):
    n = x.shape[0]
    n_tiles = n // tm
    has_pe = pe is not None
    row = lambda w: pl.BlockSpec((tm, w), lambda i: (i, 0))
    once = pl.Buffered(1)
    lay = lambda a: pl.BlockSpec((None,) + a.shape[1:], lambda i: (layer,) + (0,) * (a.ndim - 1),
                                 pipeline_mode=once)
    in_specs = [row(D_MODEL)]
    args = [x]
    if has_pe:
        pr, pc = pe
        in_specs += [pl.BlockSpec((tm // GRID_W, pr.shape[1]), lambda i: (i % (seq_len // tm), 0)),
                     pl.BlockSpec(pc.shape, lambda i: (0, 0), pipeline_mode=once)]
        args += [pr, pc]
    weights = [mod_all, w_out, gpm, gpf, gqf, w_up, cw, cb, w_down]
    in_specs += [row(FOURIER_W), row(GLA_DV_W)] + [lay(a) for a in weights]
    args += [yf, og] + weights
    return pl.pallas_call(
        functools.partial(_tail_kernel, has_pe=has_pe, per_batch=per_batch, seg=seg, tm=tm,
                          seq_len=seq_len),
        grid=(n_tiles,),
        in_specs=in_specs,
        out_specs=row(D_MODEL),
        out_shape=jax.ShapeDtypeStruct((n, D_MODEL), F32),
        scratch_shapes=[
            pltpu.VMEM((tm, D_MODEL), F32),
            pltpu.VMEM((tm, D_FF), BF16),
        ],
        compiler_params=pltpu.CompilerParams(
            dimension_semantics=("arbitrary",), vmem_limit_bytes=VMEM_LIMIT),
        name="tail",
    )(*args)


def _grid_pos_embed(rows, d):
    quarter = d // 4
    omega = 1.0 / (POS_BASE ** (jnp.arange(quarter, dtype=F32) / quarter))
    er = jnp.arange(rows, dtype=F32)[:, None] * omega
    ec = jnp.arange(GRID_W, dtype=F32)[:, None] * omega
    pr = jnp.concatenate([jnp.sin(er), jnp.cos(er)], axis=-1)
    pc = jnp.concatenate([jnp.sin(ec), jnp.cos(ec)], axis=-1)
    return pr, pc


def _dft_tables(n, scale):
    idx = jnp.arange(n, dtype=jnp.int32)
    ang = ((idx[:, None] * idx[None, :]) % n).astype(F32) * np.float32(2.0 * np.pi / n)
    return jnp.cos(ang) * np.float32(scale), jnp.sin(ang) * np.float32(scale)


def _seqdft_tables(seq_len):
    p = _dft_len(seq_len)
    radix = seq_len // p
    c, s = _dft_tables(p, seq_len ** -0.5)
    tables = (c.astype(BF16), (-s).astype(BF16))
    if radix > 1:
        k = (jnp.arange(p, dtype=jnp.int32)[:, None, None]
             + p * jnp.arange(radix, dtype=jnp.int32)[None, :, None])
        r = jnp.arange(radix, dtype=jnp.int32)[None, None, :]
        ang = ((k * r) % seq_len).astype(F32) * np.float32(2.0 * np.pi / seq_len)
        ang = ang.reshape(p, radix * radix)
        tables += (jnp.cos(ang), jnp.sin(ang))
    return tables


def _stream_layer(x, pe, mod_all, per_batch, state_in, state_out, layer, p, dft, batch, seq_len,
                  seg, tm_in, nb_dft, tm_tail):
    cc, seq_tables = dft
    ab, q, k, v, laf, lab, sg = _inproj(
        x, pe, mod_all, per_batch, layer, p['g_pre_mix'], p['w_in'], p['w_gate_f'], p['b_gate_f'],
        p['w_gate_b'], p['b_gate_b'], cc, seq_len, tm_in)
    yf = _seqdft(ab, seq_tables, batch, seq_len, nb_dft)
    og, s_t = _gla(q, k, v, laf, lab, sg, p['g_gla'], state_in, state_out, layer, batch, seq_len)
    x_new = _tail(x, pe, yf, og.reshape(batch * seq_len, GLA_DV_W), mod_all, per_batch, layer,
                  p['w_out'], p['g_post_mix'], p['g_pre_ffn'], p['g_post_ffn'], p['w_up'],
                  p['conv_w'], p['conv_b'], p['w_down'], seq_len, seg, tm_tail)
    return x_new, s_t


def kernel(x_prompt, x_sample, state_gla, c, c_ctx, g_pre_mix, g_post_mix, g_pre_ffn, g_post_ffn,
           w_ada, b_ada, w_in, w_gate_f, b_gate_f, w_gate_b, b_gate_b, g_gla, w_out, w_up, conv_w,
           conv_b, w_down):
    pb, pt, _ = x_prompt.shape
    sb, s_t, _ = x_sample.shape
    rows = s_t // GRID_W

    cv = jnp.concatenate([c_ctx[None], c, jnp.zeros((MOD_ROWS - 1 - sb, D_MODEL), F32)], axis=0)
    mod_all = _ada(cv, w_ada, b_ada)

    cc_c, cc_s = _dft_tables(FOURIER_HD, FOURIER_HD ** -0.5)
    cc = jnp.concatenate([cc_c, cc_s], axis=1).astype(BF16)
    dft_p = (cc, _seqdft_tables(pt))
    dft_s = (cc, _seqdft_tables(s_t))
    pe = _grid_pos_embed(rows, D_MODEL)

    xp = x_prompt.reshape(pb * pt, D_MODEL)
    xs = x_sample.reshape(sb * s_t, D_MODEL)
    new_state = jnp.zeros((pb, DEPTH, 2, GLA_HEADS, DK, DV), F32)
    vec = lambda a: a[:, None]
    p = dict(
        g_pre_mix=vec(g_pre_mix), g_post_mix=vec(g_post_mix), g_pre_ffn=vec(g_pre_ffn),
        g_post_ffn=vec(g_post_ffn), g_gla=vec(g_gla), w_in=w_in, w_gate_f=w_gate_f,
        b_gate_f=vec(b_gate_f), w_gate_b=w_gate_b, b_gate_b=vec(b_gate_b),
        w_out=w_out.astype(BF16), w_up=w_up.astype(BF16), conv_w=conv_w, conv_b=vec(conv_b),
        w_down=w_down.astype(BF16))
    for l in range(DEPTH):
        xp, new_state = _stream_layer(xp, None, mod_all, False, None, new_state, l, p, dft_p,
                                      pb, pt, pt, 512, 4, 512)
        xs, _ = _stream_layer(xs, pe if l == 0 else None, mod_all, True, state_gla, None, l, p,
                              dft_s, sb, s_t, GRID_W, 512, 1, 512)
    return (xp.reshape(pb, pt, D_MODEL), xs.reshape(sb, s_t, D_MODEL), new_state)
```

```python
import functools
from typing import NamedTuple

import numpy as np
import jax
import jax.numpy as jnp
from jax import lax
from jax.experimental import pallas as pl
from jax.experimental.pallas import tpu as pltpu

D_MODEL = 1024
DEPTH = 2
GRID_W = 64
FOURIER_W = 512
FOURIER_HEADS = 4
FOURIER_HD = FOURIER_W // FOURIER_HEADS
GLA_DV_W = 512
GLA_DK_W = 256
GLA_HEADS = 4
DV = GLA_DV_W // GLA_HEADS
DK = GLA_DK_W // GLA_HEADS
GATE_RANK = 16
GATE_TEMP = 16.0
CHUNK = 64
D_FF = 11 * D_MODEL // 4
EPS = 1e-6
POS_BASE = 10000.0
S_F = FOURIER_W
S_Q = S_F + GLA_DK_W
S_K = S_Q + GLA_DK_W
S_V = S_K + GLA_DV_W
S_AF = S_V + GATE_RANK
S_AB = S_AF + GATE_RANK
IN_COLS = S_AB + GLA_DV_W

C_F = 0
C_Q = C_F + FOURIER_W
C_K = C_Q + GLA_DK_W
C_V = C_K + GLA_DK_W
C_G = C_V + GLA_DV_W
C_A = C_G + GLA_DV_W
LANE = 128
IN_COLS_PAD = C_A + LANE
MOD_ROWS = 8
BLK = 256

VMEM_LIMIT = 56 * 1024 * 1024

F32 = jnp.float32
BF16 = jnp.bfloat16


def _silu(x):
    return x * jax.nn.sigmoid(x)


def _rms(x, g):
    return x * lax.rsqrt(jnp.mean(x * x, axis=-1, keepdims=True) + EPS) * g


def _dot(a, b):
    return jnp.dot(a, b, preferred_element_type=F32)


def _dot_nt(a, b):
    return lax.dot_general(a, b, (((1,), (1,)), ((), ())), preferred_element_type=F32)


def _add_pos(x, pr_ref, pc_ref):
    half = D_MODEL // 2
    reps = x.shape[0] // GRID_W
    row_part = jnp.concatenate(
        [jnp.broadcast_to(pr_ref[j:j + 1, :], (GRID_W, half)) for j in range(reps)], axis=0)
    col_part = jnp.concatenate([pc_ref[...]] * reps, axis=0)
    return x + jnp.concatenate([row_part, col_part], axis=1)


def _ada_kernel(cv_ref, w_ref, b_ref, o_ref):
    s = _silu(cv_ref[...]).astype(BF16)
    o_ref[0] = _dot(s, w_ref[0].astype(BF16)) + b_ref[0]


def _ada(cv, w_ada, b_ada):
    tn = 1536
    n_out = w_ada.shape[-1]
    return pl.pallas_call(
        _ada_kernel,
        grid=(DEPTH, n_out // tn),
        in_specs=[
            pl.BlockSpec((MOD_ROWS, D_MODEL), lambda l, n: (0, 0)),
            pl.BlockSpec((1, D_MODEL, tn), lambda l, n: (l, 0, n)),
            pl.BlockSpec((1, 1, tn), lambda l, n: (l, 0, n)),
        ],
        out_specs=pl.BlockSpec((1, MOD_ROWS, tn), lambda l, n: (l, 0, n)),
        out_shape=jax.ShapeDtypeStruct((DEPTH, MOD_ROWS, n_out), F32),
        compiler_params=pltpu.CompilerParams(
            dimension_semantics=("arbitrary", "arbitrary"), vmem_limit_bytes=VMEM_LIMIT),
        name="ada",
    )(cv, w_ada, b_ada.reshape(DEPTH, 1, n_out))


def _mod_row(tile, tm, seq_len, per_batch):
    return 1 + (tile * tm) // seq_len if per_batch else 0


def _inproj_kernel(*refs, has_pe, per_batch, radix, tm, seq_len):
    if has_pe:
        x_ref, pr_ref, pc_ref, *refs = refs
    else:
        x_ref, *refs = refs
    if radix > 1:
        *refs, ab_s = refs
    (mod_ref, g_ref, w_ref, wgf_ref, wgb_ref, bgf_ref, bgb_ref, cc_ref,
     ab_ref, q_ref, k_ref, v_ref, laf_ref, lab_ref, sg_ref, w_s, wg_s) = refs
    i = pl.program_id(0)

    def cast_weights():
        rows_per = 256
        for r0 in range(0, D_MODEL, rows_per):
            rs = slice(r0, r0 + rows_per)
            w_s[rs, C_F:C_G] = w_ref[rs, 0:S_V].astype(BF16)
            w_s[rs, C_G:C_A] = w_ref[rs, S_AB:IN_COLS].astype(BF16)
            w_s[rs, C_A:IN_COLS_PAD] = jnp.zeros((rows_per, LANE), BF16)
            w_s[rs, C_A:C_A + 2 * GATE_RANK] = w_ref[rs, S_V:S_AB].astype(BF16)
        wg_s[...] = jnp.zeros_like(wg_s)
        wg_s[0:GATE_RANK, 0:GLA_DK_W] = wgf_ref[...].astype(BF16)
        wg_s[GATE_RANK:2 * GATE_RANK, GLA_DK_W:] = wgb_ref[...].astype(BF16)

    def norm_tile():
        x = x_ref[...]
        if has_pe:
            x = _add_pos(x, pr_ref, pc_ref)
        mod = mod_ref[pl.ds(_mod_row(i, tm, seq_len, per_batch), 1), :]
        shift = mod[:, 0:D_MODEL]
        gain = g_ref[...] * (1.0 + mod[:, D_MODEL:2 * D_MODEL])
        return (_rms(x, gain) + shift).astype(BF16)

    def project_tile(h):
        z = _dot(h, w_s[...])
        zf = z[:, C_F:C_Q].astype(BF16)
        for hh in range(FOURIER_HEADS):
            sl = slice(hh * FOURIER_HD, (hh + 1) * FOURIER_HD)
            cs = _dot(zf[:, sl], cc_ref[...])
            if radix == 1:
                ab_ref[0, :, sl] = cs[:, :FOURIER_HD].astype(BF16)
                ab_ref[1, :, sl] = cs[:, FOURIER_HD:].astype(BF16)
            else:
                ab_s[0, hh] = cs[:, :FOURIER_HD]
                ab_s[1, hh] = cs[:, FOURIER_HD:]
        if radix > 1:
            for r in range(radix):
                for part in range(2):
                    for hh in range(FOURIER_HEADS):
                        c0 = r * FOURIER_W + hh * FOURIER_HD
                        ab_ref[part, :, c0:c0 + FOURIER_HD] = (
                            ab_s[part, hh, pl.ds(r, tm // radix, stride=radix), :].astype(BF16))
        q_ref[...] = z[:, C_Q:C_K] * (DK ** -0.5)
        k_ref[...] = z[:, C_K:C_V]
        v_ref[...] = z[:, C_V:C_G].astype(BF16)
        sg_ref[...] = _silu(z[:, C_G:C_A]).astype(BF16)
        xg = _dot(z[:, C_A:IN_COLS_PAD].astype(BF16), wg_s[...])
        for ref, bias, cols in ((laf_ref, bgf_ref, slice(0, GLA_DK_W)),
                                (lab_ref, bgb_ref, slice(GLA_DK_W, 2 * GLA_DK_W))):
            xb = xg[:, cols] + bias[...]
            ref[...] = (jnp.minimum(xb, 0.0) - jnp.log1p(jnp.exp(-jnp.abs(xb)))) * (1.0 / GATE_TEMP)

    pl.when(i == 0)(cast_weights)
    project_tile(norm_tile())


def _inproj(x, pe, mod_all, per_batch, layer, g, w_in, w_gate_f, b_gate_f, w_gate_b, b_gate_b, cc,
            seq_len, tm):
    n = x.shape[0]
    n_tiles = n // tm
    radix = seq_len // _dft_len(seq_len)
    has_pe = pe is not None
    once = pl.Buffered(1)
    lay = lambda a: pl.BlockSpec((None,) + a.shape[1:], lambda i: (layer,) + (0,) * (a.ndim - 1),
                                 pipeline_mode=once)
    row = lambda w: pl.BlockSpec((tm, w), lambda i: (i, 0))
    in_specs = [row(D_MODEL)]
    args = [x]
    if has_pe:
        pr, pc = pe
        in_specs += [pl.BlockSpec((tm // GRID_W, pr.shape[1]), lambda i: (i % (seq_len // tm), 0)),
                     pl.BlockSpec(pc.shape, lambda i: (0, 0), pipeline_mode=once)]
        args += [pr, pc]
    weights = [mod_all, g, w_in, w_gate_f, w_gate_b, b_gate_f, b_gate_b]
    in_specs += [lay(a) for a in weights] + [pl.BlockSpec(cc.shape, lambda i: (0, 0),
                                                          pipeline_mode=once)]
    args += weights + [cc]
    out_shape = (
        jax.ShapeDtypeStruct((2, n // radix, radix * FOURIER_W), BF16),
        jax.ShapeDtypeStruct((n, GLA_DK_W), F32),
        jax.ShapeDtypeStruct((n, GLA_DK_W), F32),
        jax.ShapeDtypeStruct((n, GLA_DV_W), BF16),
        jax.ShapeDtypeStruct((n, GLA_DK_W), F32),
        jax.ShapeDtypeStruct((n, GLA_DK_W), F32),
        jax.ShapeDtypeStruct((n, GLA_DV_W), BF16),
    )
    out_specs = (
        pl.BlockSpec((2, tm // radix, radix * FOURIER_W), lambda i: (0, i, 0)),
        row(GLA_DK_W), row(GLA_DK_W), row(GLA_DV_W), row(GLA_DK_W), row(GLA_DK_W), row(GLA_DV_W),
    )
    scratch = [pltpu.VMEM((D_MODEL, IN_COLS_PAD), BF16), pltpu.VMEM((LANE, 2 * GLA_DK_W), BF16)]
    if radix > 1:
        scratch.append(pltpu.VMEM((2, FOURIER_HEADS, tm, FOURIER_HD), F32))
    return pl.pallas_call(
        functools.partial(_inproj_kernel, has_pe=has_pe, per_batch=per_batch, radix=radix, tm=tm,
                          seq_len=seq_len),
        grid=(n_tiles,),
        in_specs=in_specs,
        out_specs=out_specs,
        out_shape=out_shape,
        scratch_shapes=scratch,
        compiler_params=pltpu.CompilerParams(
            dimension_semantics=("arbitrary",), vmem_limit_bytes=VMEM_LIMIT),
        name="inproj",
    )(*args)


DFT_MAX = 512


def _dft_len(seq_len):
    return min(seq_len, DFT_MAX)


def _seqdft_kernel(*refs, radix, nb):
    if radix > 1:
        c_ref, sn_ref, twc_ref, tws_ref, a_ref, b_ref, o_ref = refs
    else:
        c_ref, sn_ref, a_ref, b_ref, o_ref = refs
    c = c_ref[...]
    sn = sn_ref[...]
    for bi in range(nb):
        a = a_ref[bi]
        b = b_ref[bi]
        g_re = _dot(c, a) + _dot(sn, b)
        if radix == 1:
            o_ref[bi, 0] = g_re.astype(BF16)
            continue
        g_im = _dot(sn, a) - _dot(c, b)
        for k2 in range(radix):
            acc = g_re[:, 0:FOURIER_W]
            for r in range(1, radix):
                j = k2 * radix + r
                cols = slice(r * FOURIER_W, (r + 1) * FOURIER_W)
                acc = acc + twc_ref[:, j:j + 1] * g_re[:, cols] + tws_ref[:, j:j + 1] * g_im[:, cols]
            o_ref[bi, k2] = acc.astype(BF16)


def _seqdft(ab, tables, batch, seq_len, nb):
    p = _dft_len(seq_len)
    radix = seq_len // p
    width = radix * FOURIER_W
    ab4 = ab.reshape(2, batch, p, width)
    full = lambda a: pl.BlockSpec(a.shape, lambda i: (0,) * a.ndim)
    out = pl.pallas_call(
        functools.partial(_seqdft_kernel, radix=radix, nb=nb),
        grid=(batch // nb,),
        in_specs=[full(t) for t in tables] + [
            pl.BlockSpec((None, nb, p, width), lambda i: (0, i, 0, 0)),
            pl.BlockSpec((None, nb, p, width), lambda i: (1, i, 0, 0)),
        ],
        out_specs=pl.BlockSpec((nb, radix, p, FOURIER_W), lambda i: (i, 0, 0, 0)),
        out_shape=jax.ShapeDtypeStruct((batch, radix, p, FOURIER_W), BF16),
        compiler_params=pltpu.CompilerParams(
            dimension_semantics=("arbitrary",), vmem_limit_bytes=VMEM_LIMIT),
        name="seqdft",
    )(*tables, ab4, ab4)
    return out.reshape(batch * seq_len, FOURIER_W)


def _gla_kernel(*refs, seq_len, nb, has_s0, want_state):
    if has_s0:
        s0_ref, *refs = refs
    if want_state:
        _, *refs = refs
    q_ref, k_ref, v_ref, laf_ref, lab_ref, sg_ref, g_ref, og_ref, *refs = refs
    if want_state:
        sT_ref, *refs = refs
    qt_s, kt_s, ke_s, dec_s, o_s, st_s = refs
    nc = seq_len // CHUNK
    la_refs = (laf_ref, lab_ref)
    chains = [(bi, d) for bi in range(nb) for d in range(2)]

    ri = lax.broadcasted_iota(jnp.int32, (BLK, BLK), 0)
    ci = lax.broadcasted_iota(jnp.int32, (BLK, BLK), 1)
    same = (ri // CHUNK) == (ci // CHUNK)
    tri = (jnp.where(same & (ci <= ri), 1.0, 0.0).astype(BF16),
           jnp.where(same & (ci >= ri), 1.0, 0.0).astype(BF16))

    for bi, blk in [(bi, blk) for bi in range(nb) for blk in range(seq_len // BLK)]:
        rows = slice(blk * BLK, (blk + 1) * BLK)
        q = q_ref[bi, rows, :]
        k = k_ref[bi, rows, :]
        for d in range(2):
            la = la_refs[d][bi, rows, :]
            hi = la.astype(BF16)
            lo = (la - hi.astype(F32)).astype(BF16)
            b = _dot(tri[d], hi) + _dot(tri[d], lo)
            edge = CHUNK - 1 if d == 0 else 0
            tot = jnp.concatenate(
                [jnp.broadcast_to(b[cc * CHUNK + edge:cc * CHUNK + edge + 1, :], (CHUNK, GLA_DK_W))
                 for cc in range(BLK // CHUNK)], axis=0)
            qt_s[bi, d, rows, :] = (q * jnp.exp(b)).astype(BF16)
            kt_s[bi, d, rows, :] = (k * jnp.exp(-b)).astype(BF16)
            ke_s[bi, d, rows, :] = (k * jnp.exp(tot - b)).astype(BF16)
            for cc in range(BLK // CHUNK):
                c = blk * (BLK // CHUNK) + cc
                dec_s[bi, d, c:c + 1, :] = jnp.exp(tot[cc * CHUNK:cc * CHUNK + 1, :])
    for bi, d in chains:
        if has_s0:
            st_s[bi, d] = jnp.concatenate(
                [s0_ref[bi, d, h] for h in range(GLA_HEADS)], axis=0).T
        else:
            st_s[bi, d] = jnp.zeros((DV, GLA_DK_W), F32)

    lane_head = lax.broadcasted_iota(jnp.int32, (1, GLA_DK_W), 1) // DK
    head_mask = [jnp.where(lane_head == h, 1.0, 0.0).astype(BF16) for h in range(GLA_HEADS)]
    ti = lax.broadcasted_iota(jnp.int32, (CHUNK, CHUNK), 0)
    tj = lax.broadcasted_iota(jnp.int32, (CHUNK, CHUNK), 1)
    causal = (tj <= ti, tj >= ti)

    def chunk_rows(c):
        r0 = c * CHUNK
        return pl.ds(r0 if isinstance(r0, int) else pl.multiple_of(r0, CHUNK), CHUNK)

    def scan_step(step):
        chunk = (step, nc - 1 - step)
        rows = [chunk_rows(c) for c in chunk]
        v_c = {bi: [v_ref[bi, rows[d], :] for d in range(2)] for bi in range(nb)}
        res = {}
        for bi, d in chains:
            qt_c = qt_s[bi, d, rows[d], :]
            q_stack = jnp.concatenate([qt_c * head_mask[h] for h in range(GLA_HEADS)], axis=0)
            w = jnp.concatenate([st_s[bi, d].astype(BF16), kt_s[bi, d, rows[d], :]], axis=0)
            res[bi, d] = _dot_nt(q_stack, w)
        kv = {}
        for bi, d in chains:
            ke_c = ke_s[bi, d, rows[d], :]
            k_bd = jnp.concatenate([ke_c * head_mask[h] for h in range(GLA_HEADS)], axis=0)
            v_stack = jnp.concatenate(
                [v_c[bi][d][:, h * DV:(h + 1) * DV] for h in range(GLA_HEADS)], axis=0)
            v_stack_t = v_stack.astype(F32).T.astype(BF16)
            kv[bi, d] = _dot(v_stack_t, k_bd)
        for bi, d in chains:
            st_s[bi, d] = dec_s[bi, d, pl.ds(chunk[d], 1), :] * st_s[bi, d] + kv[bi, d]
        for bi, d in chains:
            for h in range(GLA_HEADS):
                hr = slice(h * CHUNK, (h + 1) * CHUNK)
                hv = slice(h * DV, (h + 1) * DV)
                att = jnp.where(causal[d], res[bi, d][hr, DV:DV + CHUNK], 0.0).astype(BF16)
                o_s[bi, d, rows[d], hv] = res[bi, d][hr, 0:DV] + _dot(att, v_c[bi][d][:, hv])

    if nc <= 4:
        for i in range(nc):
            scan_step(i)
    else:
        def body(i, carry):
            scan_step(i)
            return carry
        lax.fori_loop(0, nc, body, 0, unroll=4)

    g = g_ref[...]
    for bi in range(nb):
        for h in range(GLA_HEADS):
            hv = slice(h * DV, (h + 1) * DV)
            o = o_s[bi, 0, :, hv] + o_s[bi, 1, :, hv]
            og_ref[bi, :, hv] = (_rms(o, g) * sg_ref[bi, :, hv].astype(F32)).astype(BF16)
    if want_state:
        for bi, d in chains:
            s_d = st_s[bi, d].T
            for h in range(GLA_HEADS):
                sT_ref[bi, d, h] = s_d[h * DK:(h + 1) * DK, :]


def _gla(q, k, v, laf, lab, sg, g_gla, state_in, state_out, layer, batch, seq_len, nb):
    has_s0 = state_in is not None
    want_state = state_out is not None
    r3 = lambda a: a.reshape(batch, seq_len, a.shape[-1])
    seq = lambda w: pl.BlockSpec((nb, seq_len, w), lambda b: (b, 0, 0))
    st_spec = pl.BlockSpec((nb, None, 2, GLA_HEADS, DK, DV), lambda b: (b, layer, 0, 0, 0, 0))
    in_specs = []
    args = []
    if has_s0:
        in_specs.append(st_spec)
        args.append(state_in)
    if want_state:
        in_specs.append(pl.BlockSpec(memory_space=pl.ANY))
        args.append(state_out)
    in_specs += [seq(GLA_DK_W), seq(GLA_DK_W), seq(GLA_DV_W), seq(GLA_DK_W), seq(GLA_DK_W),
                 seq(GLA_DV_W), pl.BlockSpec((None, 1, DV), lambda b: (layer, 0, 0))]
    args += [r3(q), r3(k), r3(v), r3(laf), r3(lab), r3(sg), g_gla]
    out_specs = [seq(GLA_DV_W)]
    out_shape = [jax.ShapeDtypeStruct((batch, seq_len, GLA_DV_W), BF16)]
    if want_state:
        out_specs.append(st_spec)
        out_shape.append(jax.ShapeDtypeStruct((batch, DEPTH, 2, GLA_HEADS, DK, DV), F32))
    res = pl.pallas_call(
        functools.partial(_gla_kernel, seq_len=seq_len, nb=nb, has_s0=has_s0,
                          want_state=want_state),
        grid=(batch // nb,),
        in_specs=in_specs,
        out_specs=out_specs,
        out_shape=out_shape,
        input_output_aliases={int(has_s0): 1} if want_state else {},
        scratch_shapes=[
            pltpu.VMEM((nb, 2, seq_len, GLA_DK_W), BF16),
            pltpu.VMEM((nb, 2, seq_len, GLA_DK_W), BF16),
            pltpu.VMEM((nb, 2, seq_len, GLA_DK_W), BF16),
            pltpu.VMEM((nb, 2, seq_len // CHUNK, GLA_DK_W), F32),
            pltpu.VMEM((nb, 2, seq_len, GLA_DV_W), F32),
            pltpu.VMEM((nb, 2, DV, GLA_DK_W), F32),
        ],
        compiler_params=pltpu.CompilerParams(
            dimension_semantics=("arbitrary",), vmem_limit_bytes=VMEM_LIMIT),
        name="gla",
    )(*args)
    return res[0], (res[1] if want_state else None)


FF_CHUNK = 256


def _tail_kernel(*refs, has_pe, per_batch, seg, tm, seq_len):
    if has_pe:
        x_ref, pr_ref, pc_ref, *refs = refs
    else:
        x_ref, *refs = refs
    (yf_ref, og_ref, mod_ref, wo_ref, gpm_ref, gpf_ref, gqf_ref,
     wup_ref, cw_ref, cb_ref, wdn_ref, out_ref, x1_s, a_s) = refs
    i = pl.program_id(0)
    mod = mod_ref[pl.ds(_mod_row(i, tm, seq_len, per_batch), 1), :]

    x = x_ref[...]
    if has_pe:
        x = _add_pos(x, pr_ref, pc_ref)
    y = _dot(yf_ref[...], wo_ref[0:FOURIER_W, :]) + _dot(og_ref[...], wo_ref[FOURIER_W:, :])
    x1 = x + _rms(y, gpm_ref[...] * mod[:, 2 * D_MODEL:3 * D_MODEL])
    x1_s[...] = x1
    shift_f = mod[:, 3 * D_MODEL:4 * D_MODEL]
    gain_f = gpf_ref[...] * (1.0 + mod[:, 4 * D_MODEL:5 * D_MODEL])
    h = (_rms(x1, gain_f) + shift_f).astype(BF16)

    pos = lax.broadcasted_iota(jnp.int32, (tm, 1), 0) % seg
    first = pos == 0
    last = pos == seg - 1

    def conv_cols(c0, width):
        cols = slice(c0, c0 + width)
        u = _dot(h, wup_ref[:, cols])
        u_prev = jnp.where(first, 0.0, pltpu.roll(u, 1, 0))
        u_next = jnp.where(last, 0.0, pltpu.roll(u, tm - 1, 0))
        return (u_prev * cw_ref[0:1, cols] + u * cw_ref[1:2, cols]
                + u_next * cw_ref[2:3, cols] + cb_ref[:, cols])

    for c0 in range(0, D_FF, FF_CHUNK):
        width = min(FF_CHUNK, D_FF - c0)
        val = conv_cols(c0, width)
        gate = conv_cols(D_FF + c0, width)
        a_s[:, c0:c0 + width] = (_silu(gate) * val).astype(BF16)

    y2 = _dot(a_s[...], wdn_ref[...])
    out_ref[...] = x1_s[...] + _rms(y2, gqf_ref[...] * mod[:, 5 * D_MODEL:6 * D_MODEL])


def _tail(x, pe, yf, og, mod_all, per_batch, layer, w_out, gpm, gpf, gqf, w_up, cw, cb, w_down,
          seq_len, seg, tm):
    n = x.shape[0]
    n_tiles = n // tm
    has_pe = pe is not None
    row = lambda w: pl.BlockSpec((tm, w), lambda i: (i, 0))
    once = pl.Buffered(1)
    lay = lambda a: pl.BlockSpec((None,) + a.shape[1:], lambda i: (layer,) + (0,) * (a.ndim - 1),
                                 pipeline_mode=once)
    in_specs = [row(D_MODEL)]
    args = [x]
    if has_pe:
        pr, pc = pe
        in_specs += [pl.BlockSpec((tm // GRID_W, pr.shape[1]), lambda i: (i % (seq_len // tm), 0)),
                     pl.BlockSpec(pc.shape, lambda i: (0, 0), pipeline_mode=once)]
        args += [pr, pc]
    weights = [mod_all, w_out, gpm, gpf, gqf, w_up, cw, cb, w_down]
    in_specs += [row(FOURIER_W), row(GLA_DV_W)] + [lay(a) for a in weights]
    args += [yf, og] + weights
    return pl.pallas_call(
        functools.partial(_tail_kernel, has_pe=has_pe, per_batch=per_batch, seg=seg, tm=tm,
                          seq_len=seq_len),
        grid=(n_tiles,),
        in_specs=in_specs,
        out_specs=row(D_MODEL),
        out_shape=jax.ShapeDtypeStruct((n, D_MODEL), F32),
        scratch_shapes=[
            pltpu.VMEM((tm, D_MODEL), F32),
            pltpu.VMEM((tm, D_FF), BF16),
        ],
        compiler_params=pltpu.CompilerParams(
            dimension_semantics=("arbitrary",), vmem_limit_bytes=VMEM_LIMIT),
        name="tail",
    )(*args)


def _grid_pos_embed(rows, d):
    quarter = d // 4
    omega = 1.0 / (POS_BASE ** (jnp.arange(quarter, dtype=F32) / quarter))
    er = jnp.arange(rows, dtype=F32)[:, None] * omega
    ec = jnp.arange(GRID_W, dtype=F32)[:, None] * omega
    pr = jnp.concatenate([jnp.sin(er), jnp.cos(er)], axis=-1)
    pc = jnp.concatenate([jnp.sin(ec), jnp.cos(ec)], axis=-1)
    return pr, pc


def _dft_tables(n, scale):
    idx = jnp.arange(n, dtype=jnp.int32)
    ang = ((idx[:, None] * idx[None, :]) % n).astype(F32) * np.float32(2.0 * np.pi / n)
    return jnp.cos(ang) * np.float32(scale), jnp.sin(ang) * np.float32(scale)


def _seqdft_tables(seq_len):
    p = _dft_len(seq_len)
    radix = seq_len // p
    c, s = _dft_tables(p, seq_len ** -0.5)
    tables = (c.astype(BF16), (-s).astype(BF16))
    if radix > 1:
        k = (jnp.arange(p, dtype=jnp.int32)[:, None, None]
             + p * jnp.arange(radix, dtype=jnp.int32)[None, :, None])
        r = jnp.arange(radix, dtype=jnp.int32)[None, None, :]
        ang = ((k * r) % seq_len).astype(F32) * np.float32(2.0 * np.pi / seq_len)
        ang = ang.reshape(p, radix * radix)
        tables += (jnp.cos(ang), jnp.sin(ang))
    return tables


class _StreamCfg(NamedTuple):
    batch: int
    seq_len: int
    seg: int
    tm: int
    nb_dft: int
    nb_gla: int


def _stream_config(batch, seq_len, seg):
    return _StreamCfg(batch, seq_len, seg, tm=512,
                      nb_dft=max(1, min(batch, 1024 // seq_len)),
                      nb_gla=max(1, min(batch, 1024 // seq_len)))


def _stream_layer(x, pe, mod_all, per_batch, state_in, state_out, layer, p, dft, cfg):
    cc, seq_tables = dft
    batch, seq_len = cfg.batch, cfg.seq_len
    ab, q, k, v, laf, lab, sg = _inproj(
        x, pe, mod_all, per_batch, layer, p['g_pre_mix'], p['w_in'], p['w_gate_f'], p['b_gate_f'],
        p['w_gate_b'], p['b_gate_b'], cc, seq_len, cfg.tm)
    yf = _seqdft(ab, seq_tables, batch, seq_len, cfg.nb_dft)
    og, s_t = _gla(q, k, v, laf, lab, sg, p['g_gla'], state_in, state_out, layer, batch, seq_len,
                   cfg.nb_gla)
    x_new = _tail(x, pe, yf, og.reshape(batch * seq_len, GLA_DV_W), mod_all, per_batch, layer,
                  p['w_out'], p['g_post_mix'], p['g_pre_ffn'], p['g_post_ffn'], p['w_up'],
                  p['conv_w'], p['conv_b'], p['w_down'], seq_len, cfg.seg, cfg.tm)
    return x_new, s_t


def kernel(x_prompt, x_sample, state_gla, c, c_ctx, g_pre_mix, g_post_mix, g_pre_ffn, g_post_ffn,
           w_ada, b_ada, w_in, w_gate_f, b_gate_f, w_gate_b, b_gate_b, g_gla, w_out, w_up, conv_w,
           conv_b, w_down):
    pb, pt, _ = x_prompt.shape
    sb, s_t, _ = x_sample.shape
    rows = s_t // GRID_W

    cv = jnp.concatenate([c_ctx[None], c, jnp.zeros((MOD_ROWS - 1 - sb, D_MODEL), F32)], axis=0)
    mod_all = _ada(cv, w_ada, b_ada)

    cc_c, cc_s = _dft_tables(FOURIER_HD, FOURIER_HD ** -0.5)
    cc = jnp.concatenate([cc_c, cc_s], axis=1).astype(BF16)
    dft_p = (cc, _seqdft_tables(pt))
    dft_s = (cc, _seqdft_tables(s_t))
    pe = _grid_pos_embed(rows, D_MODEL)

    xp = x_prompt.reshape(pb * pt, D_MODEL)
    xs = x_sample.reshape(sb * s_t, D_MODEL)
    new_state = jnp.zeros((pb, DEPTH, 2, GLA_HEADS, DK, DV), F32)
    vec = lambda a: a[:, None]
    p = dict(
        g_pre_mix=vec(g_pre_mix), g_post_mix=vec(g_post_mix), g_pre_ffn=vec(g_pre_ffn),
        g_post_ffn=vec(g_post_ffn), g_gla=vec(g_gla), w_in=w_in, w_gate_f=w_gate_f,
        b_gate_f=vec(b_gate_f), w_gate_b=w_gate_b, b_gate_b=vec(b_gate_b),
        w_out=w_out.astype(BF16), w_up=w_up.astype(BF16), conv_w=conv_w, conv_b=vec(conv_b),
        w_down=w_down.astype(BF16))
    cfg_p = _stream_config(pb, pt, pt)
    cfg_s = _stream_config(sb, s_t, GRID_W)
    for l in range(DEPTH):
        xp, new_state = _stream_layer(xp, None, mod_all, False, None, new_state, l, p, dft_p,
                                      cfg_p)
        xs, _ = _stream_layer(xs, pe if l == 0 else None, mod_all, True, state_gla, None, l, p,
                              dft_s, cfg_s)
    return (xp.reshape(pb, pt, D_MODEL), xs.reshape(sb, s_t, D_MODEL), new_state)
```

```python
import functools
from typing import NamedTuple

import numpy as np
import jax
import jax.numpy as jnp
from jax import lax
from jax.experimental import pallas as pl
from jax.experimental.pallas import tpu as pltpu

D_MODEL = 1024
DEPTH = 2
GRID_W = 64
FOURIER_W = 512
FOURIER_HEADS = 4
FOURIER_HD = FOURIER_W // FOURIER_HEADS
GLA_DV_W = 512
GLA_DK_W = 256
GLA_HEADS = 4
DV = GLA_DV_W // GLA_HEADS
DK = GLA_DK_W // GLA_HEADS
GATE_RANK = 16
GATE_TEMP = 16.0
CHUNK = 64
D_FF = 11 * D_MODEL // 4
EPS = 1e-6
POS_BASE = 10000.0
S_F = FOURIER_W
S_Q = S_F + GLA_DK_W
S_K = S_Q + GLA_DK_W
S_V = S_K + GLA_DV_W
S_AF = S_V + GATE_RANK
S_AB = S_AF + GATE_RANK
IN_COLS = S_AB + GLA_DV_W

C_F = 0
C_Q = C_F + FOURIER_W
C_K = C_Q + GLA_DK_W
C_V = C_K + GLA_DK_W
C_G = C_V + GLA_DV_W
C_A = C_G + GLA_DV_W
LANE = 128
IN_COLS_PAD = C_A + LANE
MOD_ROWS = 8
BLK = 256

VMEM_LIMIT = 56 * 1024 * 1024

F32 = jnp.float32
BF16 = jnp.bfloat16


def _silu(x):
    return x * jax.nn.sigmoid(x)


def _rms(x, g):
    return x * lax.rsqrt(jnp.mean(x * x, axis=-1, keepdims=True) + EPS) * g


def _dot(a, b):
    return jnp.dot(a, b, preferred_element_type=F32)


def _dot_nt(a, b):
    return lax.dot_general(a, b, (((1,), (1,)), ((), ())), preferred_element_type=F32)


def _add_pos(x, pr_ref, pc_ref):
    half = D_MODEL // 2
    reps = x.shape[0] // GRID_W
    row_part = jnp.concatenate(
        [jnp.broadcast_to(pr_ref[j:j + 1, :], (GRID_W, half)) for j in range(reps)], axis=0)
    col_part = jnp.concatenate([pc_ref[...]] * reps, axis=0)
    return x + jnp.concatenate([row_part, col_part], axis=1)


def _ada_kernel(cv_ref, w_ref, b_ref, o_ref):
    s = _silu(cv_ref[...]).astype(BF16)
    o_ref[0] = _dot(s, w_ref[0].astype(BF16)) + b_ref[0]


def _ada(cv, w_ada, b_ada):
    tn = 1536
    n_out = w_ada.shape[-1]
    return pl.pallas_call(
        _ada_kernel,
        grid=(DEPTH, n_out // tn),
        in_specs=[
            pl.BlockSpec((MOD_ROWS, D_MODEL), lambda l, n: (0, 0)),
            pl.BlockSpec((1, D_MODEL, tn), lambda l, n: (l, 0, n)),
            pl.BlockSpec((1, 1, tn), lambda l, n: (l, 0, n)),
        ],
        out_specs=pl.BlockSpec((1, MOD_ROWS, tn), lambda l, n: (l, 0, n)),
        out_shape=jax.ShapeDtypeStruct((DEPTH, MOD_ROWS, n_out), F32),
        compiler_params=pltpu.CompilerParams(
            dimension_semantics=("arbitrary", "arbitrary"), vmem_limit_bytes=VMEM_LIMIT),
        name="ada",
    )(cv, w_ada, b_ada.reshape(DEPTH, 1, n_out))


def _mod_row(tile, tm, seq_len, per_batch):
    return 1 + (tile * tm) // seq_len if per_batch else 0


def _inproj_kernel(*refs, has_pe, per_batch, radix, tm, seq_len):
    if has_pe:
        x_ref, pr_ref, pc_ref, *refs = refs
    else:
        x_ref, *refs = refs
    if radix > 1:
        *refs, ab_s = refs
    (mod_ref, g_ref, w_ref, wgf_ref, wgb_ref, bgf_ref, bgb_ref, cc_ref,
     ab_ref, q_ref, k_ref, v_ref, laf_ref, lab_ref, sg_ref, w_s, wg_s) = refs
    i = pl.program_id(0)

    def cast_weights():
        rows_per = 256
        for r0 in range(0, D_MODEL, rows_per):
            rs = slice(r0, r0 + rows_per)
            w_s[rs, C_F:C_G] = w_ref[rs, 0:S_V].astype(BF16)
            w_s[rs, C_G:C_A] = w_ref[rs, S_AB:IN_COLS].astype(BF16)
            w_s[rs, C_A:IN_COLS_PAD] = jnp.zeros((rows_per, LANE), BF16)
            w_s[rs, C_A:C_A + 2 * GATE_RANK] = w_ref[rs, S_V:S_AB].astype(BF16)
        wg_s[...] = jnp.zeros_like(wg_s)
        wg_s[0:GATE_RANK, 0:GLA_DK_W] = wgf_ref[...].astype(BF16)
        wg_s[GATE_RANK:2 * GATE_RANK, GLA_DK_W:] = wgb_ref[...].astype(BF16)

    def norm_tile():
        x = x_ref[...]
        if has_pe:
            x = _add_pos(x, pr_ref, pc_ref)
        mod = mod_ref[pl.ds(_mod_row(i, tm, seq_len, per_batch), 1), :]
        shift = mod[:, 0:D_MODEL]
        gain = g_ref[...] * (1.0 + mod[:, D_MODEL:2 * D_MODEL])
        return (_rms(x, gain) + shift).astype(BF16)

    def project_tile(h):
        z = _dot(h, w_s[...])
        zf = z[:, C_F:C_Q].astype(BF16)
        for hh in range(FOURIER_HEADS):
            sl = slice(hh * FOURIER_HD, (hh + 1) * FOURIER_HD)
            cs = _dot(zf[:, sl], cc_ref[...])
            if radix == 1:
                ab_ref[0, :, sl] = cs[:, :FOURIER_HD].astype(BF16)
                ab_ref[1, :, sl] = cs[:, FOURIER_HD:].astype(BF16)
            else:
                ab_s[0, hh] = cs[:, :FOURIER_HD]
                ab_s[1, hh] = cs[:, FOURIER_HD:]
        if radix > 1:
            for r in range(radix):
                for part in range(2):
                    for hh in range(FOURIER_HEADS):
                        c0 = r * FOURIER_W + hh * FOURIER_HD
                        ab_ref[part, :, c0:c0 + FOURIER_HD] = (
                            ab_s[part, hh, pl.ds(r, tm // radix, stride=radix), :].astype(BF16))
        q_ref[...] = z[:, C_Q:C_K] * (DK ** -0.5)
        k_ref[...] = z[:, C_K:C_V]
        v_ref[...] = z[:, C_V:C_G].astype(BF16)
        sg_ref[...] = _silu(z[:, C_G:C_A]).astype(BF16)
        xg = _dot(z[:, C_A:IN_COLS_PAD].astype(BF16), wg_s[...])
        for ref, bias, cols in ((laf_ref, bgf_ref, slice(0, GLA_DK_W)),
                                (lab_ref, bgb_ref, slice(GLA_DK_W, 2 * GLA_DK_W))):
            xb = xg[:, cols] + bias[...]
            ref[...] = (jnp.minimum(xb, 0.0) - jnp.log1p(jnp.exp(-jnp.abs(xb)))) * (1.0 / GATE_TEMP)

    pl.when(i == 0)(cast_weights)
    project_tile(norm_tile())


def _inproj(x, pe, mod_all, per_batch, layer, g, w_in, w_gate_f, b_gate_f, w_gate_b, b_gate_b, cc,
            seq_len, tm):
    n = x.shape[0]
    n_tiles = n // tm
    radix = seq_len // _dft_len(seq_len)
    has_pe = pe is not None
    once = pl.Buffered(1)
    lay = lambda a: pl.BlockSpec((None,) + a.shape[1:], lambda i: (layer,) + (0,) * (a.ndim - 1),
                                 pipeline_mode=once)
    row = lambda w: pl.BlockSpec((tm, w), lambda i: (i, 0))
    in_specs = [row(D_MODEL)]
    args = [x]
    if has_pe:
        pr, pc = pe
        in_specs += [pl.BlockSpec((tm // GRID_W, pr.shape[1]), lambda i: (i % (seq_len // tm), 0)),
                     pl.BlockSpec(pc.shape, lambda i: (0, 0), pipeline_mode=once)]
        args += [pr, pc]
    weights = [mod_all, g, w_in, w_gate_f, w_gate_b, b_gate_f, b_gate_b]
    in_specs += [lay(a) for a in weights] + [pl.BlockSpec(cc.shape, lambda i: (0, 0),
                                                          pipeline_mode=once)]
    args += weights + [cc]
    out_shape = (
        jax.ShapeDtypeStruct((2, n // radix, radix * FOURIER_W), BF16),
        jax.ShapeDtypeStruct((n, GLA_DK_W), F32),
        jax.ShapeDtypeStruct((n, GLA_DK_W), F32),
        jax.ShapeDtypeStruct((n, GLA_DV_W), BF16),
        jax.ShapeDtypeStruct((n, GLA_DK_W), F32),
        jax.ShapeDtypeStruct((n, GLA_DK_W), F32),
        jax.ShapeDtypeStruct((n, GLA_DV_W), BF16),
    )
    out_specs = (
        pl.BlockSpec((2, tm // radix, radix * FOURIER_W), lambda i: (0, i, 0)),
        row(GLA_DK_W), row(GLA_DK_W), row(GLA_DV_W), row(GLA_DK_W), row(GLA_DK_W), row(GLA_DV_W),
    )
    scratch = [pltpu.VMEM((D_MODEL, IN_COLS_PAD), BF16), pltpu.VMEM((LANE, 2 * GLA_DK_W), BF16)]
    if radix > 1:
        scratch.append(pltpu.VMEM((2, FOURIER_HEADS, tm, FOURIER_HD), F32))
    return pl.pallas_call(
        functools.partial(_inproj_kernel, has_pe=has_pe, per_batch=per_batch, radix=radix, tm=tm,
                          seq_len=seq_len),
        grid=(n_tiles,),
        in_specs=in_specs,
        out_specs=out_specs,
        out_shape=out_shape,
        scratch_shapes=scratch,
        compiler_params=pltpu.CompilerParams(
            dimension_semantics=("arbitrary",), vmem_limit_bytes=VMEM_LIMIT),
        name="inproj",
    )(*args)


DFT_MAX = 512


def _dft_len(seq_len):
    return min(seq_len, DFT_MAX)


def _seqdft_kernel(*refs, radix, nb):
    if radix > 1:
        c_ref, sn_ref, twc_ref, tws_ref, a_ref, b_ref, o_ref = refs
    else:
        c_ref, sn_ref, a_ref, b_ref, o_ref = refs
    c = c_ref[...]
    sn = sn_ref[...]
    for bi in range(nb):
        a = a_ref[bi]
        b = b_ref[bi]
        g_re = _dot(c, a) + _dot(sn, b)
        if radix == 1:
            o_ref[bi, 0] = g_re.astype(BF16)
            continue
        g_im = _dot(sn, a) - _dot(c, b)
        for k2 in range(radix):
            acc = g_re[:, 0:FOURIER_W]
            for r in range(1, radix):
                j = k2 * radix + r
                cols = slice(r * FOURIER_W, (r + 1) * FOURIER_W)
                acc = acc + twc_ref[:, j:j + 1] * g_re[:, cols] + tws_ref[:, j:j + 1] * g_im[:, cols]
            o_ref[bi, k2] = acc.astype(BF16)


def _seqdft(ab, tables, batch, seq_len, nb):
    p = _dft_len(seq_len)
    radix = seq_len // p
    width = radix * FOURIER_W
    ab4 = ab.reshape(2, batch, p, width)
    full = lambda a: pl.BlockSpec(a.shape, lambda i: (0,) * a.ndim)
    out = pl.pallas_call(
        functools.partial(_seqdft_kernel, radix=radix, nb=nb),
        grid=(batch // nb,),
        in_specs=[full(t) for t in tables] + [
            pl.BlockSpec((None, nb, p, width), lambda i: (0, i, 0, 0)),
            pl.BlockSpec((None, nb, p, width), lambda i: (1, i, 0, 0)),
        ],
        out_specs=pl.BlockSpec((nb, radix, p, FOURIER_W), lambda i: (i, 0, 0, 0)),
        out_shape=jax.ShapeDtypeStruct((batch, radix, p, FOURIER_W), BF16),
        compiler_params=pltpu.CompilerParams(
            dimension_semantics=("arbitrary",), vmem_limit_bytes=VMEM_LIMIT),
        name="seqdft",
    )(*tables, ab4, ab4)
    return out.reshape(batch * seq_len, FOURIER_W)


UNROLLED_CHUNKS = 4


def _gla_kernel(*refs, seq_len, nb, has_s0, want_state):
    if has_s0:
        s0_ref, *refs = refs
    if want_state:
        _, *refs = refs
    q_ref, k_ref, v_ref, laf_ref, lab_ref, sg_ref, g_ref, og_ref, *refs = refs
    if want_state:
        sT_ref, *refs = refs
    qt_s, kt_s, ke_s, dec_s, o_s, st_s, *refs = refs
    nc = seq_len // CHUNK
    if nc > UNROLLED_CHUNKS:
        res_s, = refs
    la_refs = (laf_ref, lab_ref)
    chains = [(bi, d) for bi in range(nb) for d in range(2)]

    ri = lax.broadcasted_iota(jnp.int32, (BLK, BLK), 0)
    ci = lax.broadcasted_iota(jnp.int32, (BLK, BLK), 1)
    same = (ri // CHUNK) == (ci // CHUNK)
    tri = (jnp.where(same & (ci <= ri), 1.0, 0.0).astype(BF16),
           jnp.where(same & (ci >= ri), 1.0, 0.0).astype(BF16))

    for bi, blk in [(bi, blk) for bi in range(nb) for blk in range(seq_len // BLK)]:
        rows = slice(blk * BLK, (blk + 1) * BLK)
        q = q_ref[bi, rows, :]
        k = k_ref[bi, rows, :]
        for d in range(2):
            la = la_refs[d][bi, rows, :]
            hi = la.astype(BF16)
            lo = (la - hi.astype(F32)).astype(BF16)
            b = _dot(tri[d], hi) + _dot(tri[d], lo)
            edge = CHUNK - 1 if d == 0 else 0
            tot = jnp.concatenate(
                [jnp.broadcast_to(b[cc * CHUNK + edge:cc * CHUNK + edge + 1, :], (CHUNK, GLA_DK_W))
                 for cc in range(BLK // CHUNK)], axis=0)
            qt_s[bi, d, rows, :] = (q * jnp.exp(b)).astype(BF16)
            kt_s[bi, d, rows, :] = (k * jnp.exp(-b)).astype(BF16)
            ke_s[bi, d, rows, :] = (k * jnp.exp(tot - b)).astype(BF16)
            for cc in range(BLK // CHUNK):
                c = blk * (BLK // CHUNK) + cc
                dec_s[bi, d, c:c + 1, :] = jnp.exp(tot[cc * CHUNK:cc * CHUNK + 1, :])
    for bi, d in chains:
        if has_s0:
            st_s[bi, d] = jnp.concatenate(
                [s0_ref[bi, d, h] for h in range(GLA_HEADS)], axis=0).T
        else:
            st_s[bi, d] = jnp.zeros((DV, GLA_DK_W), F32)

    lane_head = lax.broadcasted_iota(jnp.int32, (1, GLA_DK_W), 1) // DK
    head_mask = [jnp.where(lane_head == h, 1.0, 0.0).astype(BF16) for h in range(GLA_HEADS)]
    ti = lax.broadcasted_iota(jnp.int32, (CHUNK, CHUNK), 0)
    tj = lax.broadcasted_iota(jnp.int32, (CHUNK, CHUNK), 1)
    causal = (tj <= ti, tj >= ti)

    def chunk_rows(c):
        r0 = c * CHUNK
        return pl.ds(r0 if isinstance(r0, int) else pl.multiple_of(r0, CHUNK), CHUNK)

    def step_rows(step):
        return [chunk_rows(step), chunk_rows(nc - 1 - step)]

    def scores_and_state(step):
        rows = step_rows(step)
        res = {}
        for bi, d in chains:
            qt_c = qt_s[bi, d, rows[d], :]
            q_stack = jnp.concatenate([qt_c * head_mask[h] for h in range(GLA_HEADS)], axis=0)
            w = jnp.concatenate([st_s[bi, d].astype(BF16), kt_s[bi, d, rows[d], :]], axis=0)
            res[bi, d] = _dot_nt(q_stack, w)
        kv = {}
        for bi, d in chains:
            ke_c = ke_s[bi, d, rows[d], :]
            k_bd = jnp.concatenate([ke_c * head_mask[h] for h in range(GLA_HEADS)], axis=0)
            v_c = v_ref[bi, rows[d], :]
            v_stack = jnp.concatenate(
                [v_c[:, h * DV:(h + 1) * DV] for h in range(GLA_HEADS)], axis=0)
            v_stack_t = v_stack.astype(F32).T.astype(BF16)
            kv[bi, d] = _dot(v_stack_t, k_bd)
        chunk = (step, nc - 1 - step)
        for bi, d in chains:
            st_s[bi, d] = dec_s[bi, d, pl.ds(chunk[d], 1), :] * st_s[bi, d] + kv[bi, d]
        return res

    def chunk_outputs(step, res):
        rows = step_rows(step)
        for bi, d in chains:
            v_c = v_ref[bi, rows[d], :]
            for h in range(GLA_HEADS):
                hr = slice(h * CHUNK, (h + 1) * CHUNK)
                hv = slice(h * DV, (h + 1) * DV)
                att = jnp.where(causal[d], res[bi, d][hr, DV:DV + CHUNK], 0.0).astype(BF16)
                o_s[bi, d, rows[d], hv] = res[bi, d][hr, 0:DV] + _dot(att, v_c[:, hv])

    if nc <= UNROLLED_CHUNKS:
        res = scores_and_state(0)
        for i in range(1, nc):
            res_next = scores_and_state(i)
            chunk_outputs(i - 1, res)
            res = res_next
        chunk_outputs(nc - 1, res)
    else:
        def hand_over(res):
            for bi, d in chains:
                res_s[bi, d] = res[bi, d]

        def body(i, carry):
            res_next = scores_and_state(i)
            chunk_outputs(i - 1, {c: res_s[c] for c in chains})
            hand_over(res_next)
            return carry

        hand_over(scores_and_state(0))
        lax.fori_loop(1, nc, body, 0, unroll=4)
        chunk_outputs(nc - 1, {c: res_s[c] for c in chains})

    g = g_ref[...]
    for bi in range(nb):
        for h in range(GLA_HEADS):
            hv = slice(h * DV, (h + 1) * DV)
            o = o_s[bi, 0, :, hv] + o_s[bi, 1, :, hv]
            og_ref[bi, :, hv] = (_rms(o, g) * sg_ref[bi, :, hv].astype(F32)).astype(BF16)
    if want_state:
        for bi, d in chains:
            s_d = st_s[bi, d].T
            for h in range(GLA_HEADS):
                sT_ref[bi, d, h] = s_d[h * DK:(h + 1) * DK, :]


def _gla(q, k, v, laf, lab, sg, g_gla, state_in, state_out, layer, batch, seq_len, nb):
    has_s0 = state_in is not None
    want_state = state_out is not None
    r3 = lambda a: a.reshape(batch, seq_len, a.shape[-1])
    seq = lambda w: pl.BlockSpec((nb, seq_len, w), lambda b: (b, 0, 0))
    st_spec = pl.BlockSpec((nb, None, 2, GLA_HEADS, DK, DV), lambda b: (b, layer, 0, 0, 0, 0))
    in_specs = []
    args = []
    if has_s0:
        in_specs.append(st_spec)
        args.append(state_in)
    if want_state:
        in_specs.append(pl.BlockSpec(memory_space=pl.ANY))
        args.append(state_out)
    in_specs += [seq(GLA_DK_W), seq(GLA_DK_W), seq(GLA_DV_W), seq(GLA_DK_W), seq(GLA_DK_W),
                 seq(GLA_DV_W), pl.BlockSpec((None, 1, DV), lambda b: (layer, 0, 0))]
    args += [r3(q), r3(k), r3(v), r3(laf), r3(lab), r3(sg), g_gla]
    out_specs = [seq(GLA_DV_W)]
    out_shape = [jax.ShapeDtypeStruct((batch, seq_len, GLA_DV_W), BF16)]
    if want_state:
        out_specs.append(st_spec)
        out_shape.append(jax.ShapeDtypeStruct((batch, DEPTH, 2, GLA_HEADS, DK, DV), F32))
    res = pl.pallas_call(
        functools.partial(_gla_kernel, seq_len=seq_len, nb=nb, has_s0=has_s0,
                          want_state=want_state),
        grid=(batch // nb,),
        in_specs=in_specs,
        out_specs=out_specs,
        out_shape=out_shape,
        input_output_aliases={int(has_s0): 1} if want_state else {},
        scratch_shapes=[
            pltpu.VMEM((nb, 2, seq_len, GLA_DK_W), BF16),
            pltpu.VMEM((nb, 2, seq_len, GLA_DK_W), BF16),
            pltpu.VMEM((nb, 2, seq_len, GLA_DK_W), BF16),
            pltpu.VMEM((nb, 2, seq_len // CHUNK, GLA_DK_W), F32),
            pltpu.VMEM((nb, 2, seq_len, GLA_DV_W), F32),
            pltpu.VMEM((nb, 2, DV, GLA_DK_W), F32),
        ] + ([pltpu.VMEM((nb, 2, GLA_HEADS * CHUNK, DV + CHUNK), F32)]
             if seq_len // CHUNK > UNROLLED_CHUNKS else []),
        compiler_params=pltpu.CompilerParams(
            dimension_semantics=("arbitrary",), vmem_limit_bytes=VMEM_LIMIT),
        name="gla",
    )(*args)
    return res[0], (res[1] if want_state else None)


FF_CHUNK = 256


def _tail_kernel(*refs, has_pe, per_batch, seg, tm, seq_len):
    if has_pe:
        x_ref, pr_ref, pc_ref, *refs = refs
    else:
        x_ref, *refs = refs
    (yf_ref, og_ref, mod_ref, wo_ref, gpm_ref, gpf_ref, gqf_ref,
     wup_ref, cw_ref, cb_ref, wdn_ref, out_ref, x1_s, a_s) = refs
    i = pl.program_id(0)
    mod = mod_ref[pl.ds(_mod_row(i, tm, seq_len, per_batch), 1), :]

    x = x_ref[...]
    if has_pe:
        x = _add_pos(x, pr_ref, pc_ref)
    y = _dot(yf_ref[...], wo_ref[0:FOURIER_W, :]) + _dot(og_ref[...], wo_ref[FOURIER_W:, :])
    x1 = x + _rms(y, gpm_ref[...] * mod[:, 2 * D_MODEL:3 * D_MODEL])
    x1_s[...] = x1
    shift_f = mod[:, 3 * D_MODEL:4 * D_MODEL]
    gain_f = gpf_ref[...] * (1.0 + mod[:, 4 * D_MODEL:5 * D_MODEL])
    h = (_rms(x1, gain_f) + shift_f).astype(BF16)

    pos = lax.broadcasted_iota(jnp.int32, (tm, 1), 0) % seg
    first = pos == 0
    last = pos == seg - 1

    def conv_cols(c0, width):
        cols = slice(c0, c0 + width)
        u = _dot(h, wup_ref[:, cols])
        u_prev = jnp.where(first, 0.0, pltpu.roll(u, 1, 0))
        u_next = jnp.where(last, 0.0, pltpu.roll(u, tm - 1, 0))
        return (u_prev * cw_ref[0:1, cols] + u * cw_ref[1:2, cols]
                + u_next * cw_ref[2:3, cols] + cb_ref[:, cols])

    for c0 in range(0, D_FF, FF_CHUNK):
        width = min(FF_CHUNK, D_FF - c0)
        val = conv_cols(c0, width)
        gate = conv_cols(D_FF + c0, width)
        a_s[:, c0:c0 + width] = (_silu(gate) * val).astype(BF16)

    y2 = _dot(a_s[...], wdn_ref[...])
    out_ref[...] = x1_s[...] + _rms(y2, gqf_ref[...] * mod[:, 5 * D_MODEL:6 * D_MODEL])


def _tail(x, pe, yf, og, mod_all, per_batch, layer, w_out, gpm, gpf, gqf, w_up, cw, cb, w_down,
          seq_len, seg, tm):
    n = x.shape[0]
    n_tiles = n // tm
    has_pe = pe is not None
    row = lambda w: pl.BlockSpec((tm, w), lambda i: (i, 0))
    once = pl.Buffered(1)
    lay = lambda a: pl.BlockSpec((None,) + a.shape[1:], lambda i: (layer,) + (0,) * (a.ndim - 1),
                                 pipeline_mode=once)
    in_specs = [row(D_MODEL)]
    args = [x]
    if has_pe:
        pr, pc = pe
        in_specs += [pl.BlockSpec((tm // GRID_W, pr.shape[1]), lambda i: (i % (seq_len // tm), 0)),
                     pl.BlockSpec(pc.shape, lambda i: (0, 0), pipeline_mode=once)]
        args += [pr, pc]
    weights = [mod_all, w_out, gpm, gpf, gqf, w_up, cw, cb, w_down]
    in_specs += [row(FOURIER_W), row(GLA_DV_W)] + [lay(a) for a in weights]
    args += [yf, og] + weights
    return pl.pallas_call(
        functools.partial(_tail_kernel, has_pe=has_pe, per_batch=per_batch, seg=seg, tm=tm,
                          seq_len=seq_len),
        grid=(n_tiles,),
        in_specs=in_specs,
        out_specs=row(D_MODEL),
        out_shape=jax.ShapeDtypeStruct((n, D_MODEL), F32),
        scratch_shapes=[
            pltpu.VMEM((tm, D_MODEL), F32),
            pltpu.VMEM((tm, D_FF), BF16),
        ],
        compiler_params=pltpu.CompilerParams(
            dimension_semantics=("arbitrary",), vmem_limit_bytes=VMEM_LIMIT),
        name="tail",
    )(*args)


def _grid_pos_embed(rows, d):
    quarter = d // 4
    omega = 1.0 / (POS_BASE ** (jnp.arange(quarter, dtype=F32) / quarter))
    er = jnp.arange(rows, dtype=F32)[:, None] * omega
    ec = jnp.arange(GRID_W, dtype=F32)[:, None] * omega
    pr = jnp.concatenate([jnp.sin(er), jnp.cos(er)], axis=-1)
    pc = jnp.concatenate([jnp.sin(ec), jnp.cos(ec)], axis=-1)
    return pr, pc


def _dft_tables(n, scale):
    idx = jnp.arange(n, dtype=jnp.int32)
    ang = ((idx[:, None] * idx[None, :]) % n).astype(F32) * np.float32(2.0 * np.pi / n)
    return jnp.cos(ang) * np.float32(scale), jnp.sin(ang) * np.float32(scale)


def _seqdft_tables(seq_len):
    p = _dft_len(seq_len)
    radix = seq_len // p
    c, s = _dft_tables(p, seq_len ** -0.5)
    tables = (c.astype(BF16), (-s).astype(BF16))
    if radix > 1:
        k = (jnp.arange(p, dtype=jnp.int32)[:, None, None]
             + p * jnp.arange(radix, dtype=jnp.int32)[None, :, None])
        r = jnp.arange(radix, dtype=jnp.int32)[None, None, :]
        ang = ((k * r) % seq_len).astype(F32) * np.float32(2.0 * np.pi / seq_len)
        ang = ang.reshape(p, radix * radix)
        tables += (jnp.cos(ang), jnp.sin(ang))
    return tables


class _StreamCfg(NamedTuple):
    batch: int
    seq_len: int
    seg: int
    tm: int
    nb_dft: int
    nb_gla: int


def _stream_config(batch, seq_len, seg):
    return _StreamCfg(batch, seq_len, seg, tm=512,
                      nb_dft=max(1, min(batch, 1024 // seq_len)),
                      nb_gla=max(1, min(batch, 1024 // seq_len)))


def _stream_layer(x, pe, mod_all, per_batch, state_in, state_out, layer, p, dft, cfg):
    cc, seq_tables = dft
    batch, seq_len = cfg.batch, cfg.seq_len
    ab, q, k, v, laf, lab, sg = _inproj(
        x, pe, mod_all, per_batch, layer, p['g_pre_mix'], p['w_in'], p['w_gate_f'], p['b_gate_f'],
        p['w_gate_b'], p['b_gate_b'], cc, seq_len, cfg.tm)
    yf = _seqdft(ab, seq_tables, batch, seq_len, cfg.nb_dft)
    og, s_t = _gla(q, k, v, laf, lab, sg, p['g_gla'], state_in, state_out, layer, batch, seq_len,
                   cfg.nb_gla)
    x_new = _tail(x, pe, yf, og.reshape(batch * seq_len, GLA_DV_W), mod_all, per_batch, layer,
                  p['w_out'], p['g_post_mix'], p['g_pre_ffn'], p['g_post_ffn'], p['w_up'],
                  p['conv_w'], p['conv_b'], p['w_down'], seq_len, cfg.seg, cfg.tm)
    return x_new, s_t


def kernel(x_prompt, x_sample, state_gla, c, c_ctx, g_pre_mix, g_post_mix, g_pre_ffn, g_post_ffn,
           w_ada, b_ada, w_in, w_gate_f, b_gate_f, w_gate_b, b_gate_b, g_gla, w_out, w_up, conv_w,
           conv_b, w_down):
    pb, pt, _ = x_prompt.shape
    sb, s_t, _ = x_sample.shape
    rows = s_t // GRID_W

    cv = jnp.concatenate([c_ctx[None], c, jnp.zeros((MOD_ROWS - 1 - sb, D_MODEL), F32)], axis=0)
    mod_all = _ada(cv, w_ada, b_ada)

    cc_c, cc_s = _dft_tables(FOURIER_HD, FOURIER_HD ** -0.5)
    cc = jnp.concatenate([cc_c, cc_s], axis=1).astype(BF16)
    dft_p = (cc, _seqdft_tables(pt))
    dft_s = (cc, _seqdft_tables(s_t))
    pe = _grid_pos_embed(rows, D_MODEL)

    xp = x_prompt.reshape(pb * pt, D_MODEL)
    xs = x_sample.reshape(sb * s_t, D_MODEL)
    new_state = jnp.zeros((pb, DEPTH, 2, GLA_HEADS, DK, DV), F32)
    vec = lambda a: a[:, None]
    p = dict(
        g_pre_mix=vec(g_pre_mix), g_post_mix=vec(g_post_mix), g_pre_ffn=vec(g_pre_ffn),
        g_post_ffn=vec(g_post_ffn), g_gla=vec(g_gla), w_in=w_in, w_gate_f=w_gate_f,
        b_gate_f=vec(b_gate_f), w_gate_b=w_gate_b, b_gate_b=vec(b_gate_b),
        w_out=w_out.astype(BF16), w_up=w_up.astype(BF16), conv_w=conv_w, conv_b=vec(conv_b),
        w_down=w_down.astype(BF16))
    cfg_p = _stream_config(pb, pt, pt)
    cfg_s = _stream_config(sb, s_t, GRID_W)
    for l in range(DEPTH):
        xp, new_state = _stream_layer(xp, None, mod_all, False, None, new_state, l, p, dft_p,
                                      cfg_p)
        xs, _ = _stream_layer(xs, pe if l == 0 else None, mod_all, True, state_gla, None, l, p,
                              dft_s, cfg_s)
    return (xp.reshape(pb, pt, D_MODEL), xs.reshape(sb, s_t, D_MODEL), new_state)
```

```python
import functools
from typing import NamedTuple

import numpy as np
import jax
import jax.numpy as jnp
from jax import lax
from jax.experimental import pallas as pl
from jax.experimental.pallas import tpu as pltpu

D_MODEL = 1024
DEPTH = 2
GRID_W = 64
FOURIER_W = 512
FOURIER_HEADS = 4
FOURIER_HD = FOURIER_W // FOURIER_HEADS
GLA_DV_W = 512
GLA_DK_W = 256
GLA_HEADS = 4
DV = GLA_DV_W // GLA_HEADS
DK = GLA_DK_W // GLA_HEADS
GATE_RANK = 16
GATE_TEMP = 16.0
CHUNK = 64
D_FF = 11 * D_MODEL // 4
EPS = 1e-6
POS_BASE = 10000.0
S_F = FOURIER_W
S_Q = S_F + GLA_DK_W
S_K = S_Q + GLA_DK_W
S_V = S_K + GLA_DV_W
S_AF = S_V + GATE_RANK
S_AB = S_AF + GATE_RANK
IN_COLS = S_AB + GLA_DV_W

C_F = 0
C_Q = C_F + FOURIER_W
C_K = C_Q + GLA_DK_W
C_V = C_K + GLA_DK_W
C_G = C_V + GLA_DV_W
C_A = C_G + GLA_DV_W
LANE = 128
IN_COLS_PAD = C_A + LANE
MOD_ROWS = 8
BLK = 256

VMEM_LIMIT = 56 * 1024 * 1024

F32 = jnp.float32
BF16 = jnp.bfloat16


def _silu(x):
    return x * jax.nn.sigmoid(x)


def _rms(x, g):
    return x * lax.rsqrt(jnp.mean(x * x, axis=-1, keepdims=True) + EPS) * g


def _dot(a, b):
    return jnp.dot(a, b, preferred_element_type=F32)


def _dot_nt(a, b):
    return lax.dot_general(a, b, (((1,), (1,)), ((), ())), preferred_element_type=F32)


def _add_pos(x, pr_ref, pc_ref):
    half = D_MODEL // 2
    reps = x.shape[0] // GRID_W
    row_part = jnp.concatenate(
        [jnp.broadcast_to(pr_ref[j:j + 1, :], (GRID_W, half)) for j in range(reps)], axis=0)
    col_part = jnp.concatenate([pc_ref[...]] * reps, axis=0)
    return x + jnp.concatenate([row_part, col_part], axis=1)


def _ada_kernel(cv_ref, w_ref, b_ref, o_ref):
    s = _silu(cv_ref[...]).astype(BF16)
    o_ref[0] = _dot(s, w_ref[0].astype(BF16)) + b_ref[0]


def _ada(cv, w_ada, b_ada):
    tn = 1536
    n_out = w_ada.shape[-1]
    return pl.pallas_call(
        _ada_kernel,
        grid=(DEPTH, n_out // tn),
        in_specs=[
            pl.BlockSpec((MOD_ROWS, D_MODEL), lambda l, n: (0, 0)),
            pl.BlockSpec((1, D_MODEL, tn), lambda l, n: (l, 0, n)),
            pl.BlockSpec((1, 1, tn), lambda l, n: (l, 0, n)),
        ],
        out_specs=pl.BlockSpec((1, MOD_ROWS, tn), lambda l, n: (l, 0, n)),
        out_shape=jax.ShapeDtypeStruct((DEPTH, MOD_ROWS, n_out), F32),
        compiler_params=pltpu.CompilerParams(
            dimension_semantics=("arbitrary", "arbitrary"), vmem_limit_bytes=VMEM_LIMIT),
        name="ada",
    )(cv, w_ada, b_ada.reshape(DEPTH, 1, n_out))


def _mod_row(tile, tm, seq_len, per_batch):
    return 1 + (tile * tm) // seq_len if per_batch else 0


def _inproj_kernel(*refs, has_pe, per_batch, radix, tm, seq_len, n_cast):
    if has_pe:
        x_ref, pr_ref, pc_ref, *refs = refs
    else:
        x_ref, *refs = refs
    if radix > 1:
        *refs, ab_s = refs
    mod_ref, g_ref, w_ref, wgf_ref, wgb_ref, bgf_ref, bgb_ref, cc_ref, *refs = refs
    cast_in, refs = refs[:n_cast], refs[n_cast:]
    ab_ref, q_ref, k_ref, v_ref, laf_ref, lab_ref, sg_ref, *refs = refs
    cast_out, (w_s, wg_s) = refs[:n_cast], refs[n_cast:]
    i = pl.program_id(0)

    def cast_weights():
        rows_per = 256
        for r0 in range(0, D_MODEL, rows_per):
            rs = slice(r0, r0 + rows_per)
            w_s[rs, C_F:C_G] = w_ref[rs, 0:S_V].astype(BF16)
            w_s[rs, C_G:C_A] = w_ref[rs, S_AB:IN_COLS].astype(BF16)
            w_s[rs, C_A:IN_COLS_PAD] = jnp.zeros((rows_per, LANE), BF16)
            w_s[rs, C_A:C_A + 2 * GATE_RANK] = w_ref[rs, S_V:S_AB].astype(BF16)
        wg_s[...] = jnp.zeros_like(wg_s)
        wg_s[0:GATE_RANK, 0:GLA_DK_W] = wgf_ref[...].astype(BF16)
        wg_s[GATE_RANK:2 * GATE_RANK, GLA_DK_W:] = wgb_ref[...].astype(BF16)

    def norm_tile():
        x = x_ref[...]
        if has_pe:
            x = _add_pos(x, pr_ref, pc_ref)
        mod = mod_ref[pl.ds(_mod_row(i, tm, seq_len, per_batch), 1), :]
        shift = mod[:, 0:D_MODEL]
        gain = g_ref[...] * (1.0 + mod[:, D_MODEL:2 * D_MODEL])
        return (_rms(x, gain) + shift).astype(BF16)

    def project_tile(h):
        z = _dot(h, w_s[...])
        zf = z[:, C_F:C_Q].astype(BF16)
        for hh in range(FOURIER_HEADS):
            sl = slice(hh * FOURIER_HD, (hh + 1) * FOURIER_HD)
            cs = _dot(zf[:, sl], cc_ref[...])
            if radix == 1:
                ab_ref[0, :, sl] = cs[:, :FOURIER_HD].astype(BF16)
                ab_ref[1, :, sl] = cs[:, FOURIER_HD:].astype(BF16)
            else:
                ab_s[0, hh] = cs[:, :FOURIER_HD]
                ab_s[1, hh] = cs[:, FOURIER_HD:]
        if radix > 1:
            for r in range(radix):
                for part in range(2):
                    for hh in range(FOURIER_HEADS):
                        c0 = r * FOURIER_W + hh * FOURIER_HD
                        ab_ref[part, :, c0:c0 + FOURIER_HD] = (
                            ab_s[part, hh, pl.ds(r, tm // radix, stride=radix), :].astype(BF16))
        q_ref[...] = z[:, C_Q:C_K] * (DK ** -0.5)
        k_ref[...] = z[:, C_K:C_V]
        v_ref[...] = z[:, C_V:C_G].astype(BF16)
        sg_ref[...] = _silu(z[:, C_G:C_A]).astype(BF16)
        xg = _dot(z[:, C_A:IN_COLS_PAD].astype(BF16), wg_s[...])
        for ref, bias, cols in ((laf_ref, bgf_ref, slice(0, GLA_DK_W)),
                                (lab_ref, bgb_ref, slice(GLA_DK_W, 2 * GLA_DK_W))):
            xb = xg[:, cols] + bias[...]
            ref[...] = (jnp.minimum(xb, 0.0) - jnp.log1p(jnp.exp(-jnp.abs(xb)))) * (1.0 / GATE_TEMP)

    pl.when(i == 0)(cast_weights)
    project_tile(norm_tile())
    for src, dst in zip(cast_in, cast_out):
        dst[...] = src[...].astype(BF16)


def _inproj(x, pe, mod_all, per_batch, layer, g, w_in, w_gate_f, b_gate_f, w_gate_b, b_gate_b, cc,
            seq_len, tm, cast=()):
    n = x.shape[0]
    n_tiles = n // tm
    radix = seq_len // _dft_len(seq_len)
    has_pe = pe is not None
    once = pl.Buffered(1)
    lay = lambda a: pl.BlockSpec((None,) + a.shape[1:], lambda i: (layer,) + (0,) * (a.ndim - 1),
                                 pipeline_mode=once)
    row = lambda w: pl.BlockSpec((tm, w), lambda i: (i, 0))
    in_specs = [row(D_MODEL)]
    args = [x]
    if has_pe:
        pr, pc = pe
        in_specs += [pl.BlockSpec((tm // GRID_W, pr.shape[1]), lambda i: (i % (seq_len // tm), 0)),
                     pl.BlockSpec(pc.shape, lambda i: (0, 0), pipeline_mode=once)]
        args += [pr, pc]
    weights = [mod_all, g, w_in, w_gate_f, w_gate_b, b_gate_f, b_gate_b]
    in_specs += [lay(a) for a in weights] + [pl.BlockSpec(cc.shape, lambda i: (0, 0),
                                                          pipeline_mode=once)]
    args += weights + [cc]
    assert all(a.shape[1] % n_tiles == 0 for a in cast)
    cast_rows = [a.shape[1] // n_tiles for a in cast]
    in_specs += [pl.BlockSpec((None, r, a.shape[2]), lambda i: (layer, i, 0))
                 for a, r in zip(cast, cast_rows)]
    args += list(cast)
    out_shape = (
        jax.ShapeDtypeStruct((2, n // radix, radix * FOURIER_W), BF16),
        jax.ShapeDtypeStruct((n, GLA_DK_W), F32),
        jax.ShapeDtypeStruct((n, GLA_DK_W), F32),
        jax.ShapeDtypeStruct((n, GLA_DV_W), BF16),
        jax.ShapeDtypeStruct((n, GLA_DK_W), F32),
        jax.ShapeDtypeStruct((n, GLA_DK_W), F32),
        jax.ShapeDtypeStruct((n, GLA_DV_W), BF16),
    )
    out_specs = (
        pl.BlockSpec((2, tm // radix, radix * FOURIER_W), lambda i: (0, i, 0)),
        row(GLA_DK_W), row(GLA_DK_W), row(GLA_DV_W), row(GLA_DK_W), row(GLA_DK_W), row(GLA_DV_W),
    )
    out_shape += tuple(jax.ShapeDtypeStruct((1,) + a.shape[1:], BF16) for a in cast)
    out_specs += tuple(pl.BlockSpec((None, r, a.shape[2]), lambda i: (0, i, 0))
                       for a, r in zip(cast, cast_rows))
    scratch = [pltpu.VMEM((D_MODEL, IN_COLS_PAD), BF16), pltpu.VMEM((LANE, 2 * GLA_DK_W), BF16)]
    if radix > 1:
        scratch.append(pltpu.VMEM((2, FOURIER_HEADS, tm, FOURIER_HD), F32))
    outs = pl.pallas_call(
        functools.partial(_inproj_kernel, has_pe=has_pe, per_batch=per_batch, radix=radix, tm=tm,
                          seq_len=seq_len, n_cast=len(cast)),
        grid=(n_tiles,),
        in_specs=in_specs,
        out_specs=out_specs,
        out_shape=out_shape,
        scratch_shapes=scratch,
        compiler_params=pltpu.CompilerParams(
            dimension_semantics=("arbitrary",), vmem_limit_bytes=VMEM_LIMIT),
        name="inproj",
    )(*args)
    return outs[:7], outs[7:]


DFT_MAX = 512


def _dft_len(seq_len):
    return min(seq_len, DFT_MAX)


def _seqdft_kernel(*refs, radix, nb):
    if radix > 1:
        c_ref, sn_ref, twc_ref, tws_ref, a_ref, b_ref, o_ref = refs
    else:
        c_ref, sn_ref, a_ref, b_ref, o_ref = refs
    c = c_ref[...]
    sn = sn_ref[...]
    for bi in range(nb):
        a = a_ref[bi]
        b = b_ref[bi]
        g_re = _dot(c, a) + _dot(sn, b)
        if radix == 1:
            o_ref[bi, 0] = g_re.astype(BF16)
            continue
        g_im = _dot(sn, a) - _dot(c, b)
        for k2 in range(radix):
            acc = g_re[:, 0:FOURIER_W]
            for r in range(1, radix):
                j = k2 * radix + r
                cols = slice(r * FOURIER_W, (r + 1) * FOURIER_W)
                acc = acc + twc_ref[:, j:j + 1] * g_re[:, cols] + tws_ref[:, j:j + 1] * g_im[:, cols]
            o_ref[bi, k2] = acc.astype(BF16)


def _seqdft(ab, tables, batch, seq_len, nb):
    p = _dft_len(seq_len)
    radix = seq_len // p
    width = radix * FOURIER_W
    ab4 = ab.reshape(2, batch, p, width)
    full = lambda a: pl.BlockSpec(a.shape, lambda i: (0,) * a.ndim)
    out = pl.pallas_call(
        functools.partial(_seqdft_kernel, radix=radix, nb=nb),
        grid=(batch // nb,),
        in_specs=[full(t) for t in tables] + [
            pl.BlockSpec((None, nb, p, width), lambda i: (0, i, 0, 0)),
            pl.BlockSpec((None, nb, p, width), lambda i: (1, i, 0, 0)),
        ],
        out_specs=pl.BlockSpec((nb, radix, p, FOURIER_W), lambda i: (i, 0, 0, 0)),
        out_shape=jax.ShapeDtypeStruct((batch, radix, p, FOURIER_W), BF16),
        compiler_params=pltpu.CompilerParams(
            dimension_semantics=("arbitrary",), vmem_limit_bytes=VMEM_LIMIT),
        name="seqdft",
    )(*tables, ab4, ab4)
    return out.reshape(batch * seq_len, FOURIER_W)


def _gla_kernel(*refs, seq_len, nb, has_s0, want_state):
    if has_s0:
        s0_ref, *refs = refs
    if want_state:
        _, *refs = refs
    q_ref, k_ref, v_ref, laf_ref, lab_ref, sg_ref, g_ref, og_ref, *refs = refs
    if want_state:
        sT_ref, *refs = refs
    qt_s, kt_s, ke_s, dec_s, o_s, st_s = refs
    nc = seq_len // CHUNK
    la_refs = (laf_ref, lab_ref)
    chains = [(bi, d) for bi in range(nb) for d in range(2)]

    ri = lax.broadcasted_iota(jnp.int32, (BLK, BLK), 0)
    ci = lax.broadcasted_iota(jnp.int32, (BLK, BLK), 1)
    same = (ri // CHUNK) == (ci // CHUNK)
    tri = (jnp.where(same & (ci <= ri), 1.0, 0.0).astype(BF16),
           jnp.where(same & (ci >= ri), 1.0, 0.0).astype(BF16))

    for bi, blk in [(bi, blk) for bi in range(nb) for blk in range(seq_len // BLK)]:
        rows = slice(blk * BLK, (blk + 1) * BLK)
        q = q_ref[bi, rows, :]
        k = k_ref[bi, rows, :]
        for d in range(2):
            la = la_refs[d][bi, rows, :]
            hi = la.astype(BF16)
            lo = (la - hi.astype(F32)).astype(BF16)
            b = _dot(tri[d], hi) + _dot(tri[d], lo)
            edge = CHUNK - 1 if d == 0 else 0
            tot = jnp.concatenate(
                [jnp.broadcast_to(b[cc * CHUNK + edge:cc * CHUNK + edge + 1, :], (CHUNK, GLA_DK_W))
                 for cc in range(BLK // CHUNK)], axis=0)
            qt_s[bi, d, rows, :] = (q * jnp.exp(b)).astype(BF16)
            kt_s[bi, d, rows, :] = (k * jnp.exp(-b)).astype(BF16)
            ke_s[bi, d, rows, :] = (k * jnp.exp(tot - b)).astype(BF16)
            for cc in range(BLK // CHUNK):
                c = blk * (BLK // CHUNK) + cc
                dec_s[bi, d, c:c + 1, :] = jnp.exp(tot[cc * CHUNK:cc * CHUNK + 1, :])
    for bi, d in chains:
        if has_s0:
            st_s[bi, d] = jnp.concatenate(
                [s0_ref[bi, d, h] for h in range(GLA_HEADS)], axis=0).T
        else:
            st_s[bi, d] = jnp.zeros((DV, GLA_DK_W), F32)

    lane_head = lax.broadcasted_iota(jnp.int32, (1, GLA_DK_W), 1) // DK
    head_mask = [jnp.where(lane_head == h, 1.0, 0.0).astype(BF16) for h in range(GLA_HEADS)]
    ti = lax.broadcasted_iota(jnp.int32, (CHUNK, CHUNK), 0)
    tj = lax.broadcasted_iota(jnp.int32, (CHUNK, CHUNK), 1)
    causal = (tj <= ti, tj >= ti)

    def chunk_rows(c):
        r0 = c * CHUNK
        return pl.ds(r0 if isinstance(r0, int) else pl.multiple_of(r0, CHUNK), CHUNK)

    def scan_step(step):
        chunk = (step, nc - 1 - step)
        rows = [chunk_rows(c) for c in chunk]
        v_c = {bi: [v_ref[bi, rows[d], :] for d in range(2)] for bi in range(nb)}
        res = {}
        for bi, d in chains:
            qt_c = qt_s[bi, d, rows[d], :]
            q_stack = jnp.concatenate([qt_c * head_mask[h] for h in range(GLA_HEADS)], axis=0)
            w = jnp.concatenate([st_s[bi, d].astype(BF16), kt_s[bi, d, rows[d], :]], axis=0)
            res[bi, d] = _dot_nt(q_stack, w)
        kv = {}
        for bi, d in chains:
            ke_c = ke_s[bi, d, rows[d], :]
            k_bd = jnp.concatenate([ke_c * head_mask[h] for h in range(GLA_HEADS)], axis=0)
            v_stack = jnp.concatenate(
                [v_c[bi][d][:, h * DV:(h + 1) * DV] for h in range(GLA_HEADS)], axis=0)
            v_stack_t = v_stack.astype(F32).T.astype(BF16)
            kv[bi, d] = _dot(v_stack_t, k_bd)
        for bi, d in chains:
            st_s[bi, d] = dec_s[bi, d, pl.ds(chunk[d], 1), :] * st_s[bi, d] + kv[bi, d]
        for bi, d in chains:
            for h in range(GLA_HEADS):
                hr = slice(h * CHUNK, (h + 1) * CHUNK)
                hv = slice(h * DV, (h + 1) * DV)
                att = jnp.where(causal[d], res[bi, d][hr, DV:DV + CHUNK], 0.0).astype(BF16)
                o_s[bi, d, rows[d], hv] = res[bi, d][hr, 0:DV] + _dot(att, v_c[bi][d][:, hv])

    if nc <= 4:
        for i in range(nc):
            scan_step(i)
    else:
        def body(i, carry):
            scan_step(i)
            return carry
        lax.fori_loop(0, nc, body, 0, unroll=4)

    g = g_ref[...]
    for bi in range(nb):
        for h in range(GLA_HEADS):
            hv = slice(h * DV, (h + 1) * DV)
            o = o_s[bi, 0, :, hv] + o_s[bi, 1, :, hv]
            og_ref[bi, :, hv] = (_rms(o, g) * sg_ref[bi, :, hv].astype(F32)).astype(BF16)
    if want_state:
        for bi, d in chains:
            s_d = st_s[bi, d].T
            for h in range(GLA_HEADS):
                sT_ref[bi, d, h] = s_d[h * DK:(h + 1) * DK, :]


def _gla(q, k, v, laf, lab, sg, g_gla, state_in, state_out, layer, batch, seq_len, nb):
    has_s0 = state_in is not None
    want_state = state_out is not None
    r3 = lambda a: a.reshape(batch, seq_len, a.shape[-1])
    seq = lambda w: pl.BlockSpec((nb, seq_len, w), lambda b: (b, 0, 0))
    st_spec = pl.BlockSpec((nb, None, 2, GLA_HEADS, DK, DV), lambda b: (b, layer, 0, 0, 0, 0))
    in_specs = []
    args = []
    if has_s0:
        in_specs.append(st_spec)
        args.append(state_in)
    if want_state:
        in_specs.append(pl.BlockSpec(memory_space=pl.ANY))
        args.append(state_out)
    in_specs += [seq(GLA_DK_W), seq(GLA_DK_W), seq(GLA_DV_W), seq(GLA_DK_W), seq(GLA_DK_W),
                 seq(GLA_DV_W), pl.BlockSpec((None, 1, DV), lambda b: (layer, 0, 0))]
    args += [r3(q), r3(k), r3(v), r3(laf), r3(lab), r3(sg), g_gla]
    out_specs = [seq(GLA_DV_W)]
    out_shape = [jax.ShapeDtypeStruct((batch, seq_len, GLA_DV_W), BF16)]
    if want_state:
        out_specs.append(st_spec)
        out_shape.append(jax.ShapeDtypeStruct((batch, DEPTH, 2, GLA_HEADS, DK, DV), F32))
    res = pl.pallas_call(
        functools.partial(_gla_kernel, seq_len=seq_len, nb=nb, has_s0=has_s0,
                          want_state=want_state),
        grid=(batch // nb,),
        in_specs=in_specs,
        out_specs=out_specs,
        out_shape=out_shape,
        input_output_aliases={int(has_s0): 1} if want_state else {},
        scratch_shapes=[
            pltpu.VMEM((nb, 2, seq_len, GLA_DK_W), BF16),
            pltpu.VMEM((nb, 2, seq_len, GLA_DK_W), BF16),
            pltpu.VMEM((nb, 2, seq_len, GLA_DK_W), BF16),
            pltpu.VMEM((nb, 2, seq_len // CHUNK, GLA_DK_W), F32),
            pltpu.VMEM((nb, 2, seq_len, GLA_DV_W), F32),
            pltpu.VMEM((nb, 2, DV, GLA_DK_W), F32),
        ],
        compiler_params=pltpu.CompilerParams(
            dimension_semantics=("arbitrary",), vmem_limit_bytes=VMEM_LIMIT),
        name="gla",
    )(*args)
    return res[0], (res[1] if want_state else None)


FF_CHUNK = 256


def _tail_kernel(*refs, has_pe, per_batch, seg, tm, seq_len):
    if has_pe:
        x_ref, pr_ref, pc_ref, *refs = refs
    else:
        x_ref, *refs = refs
    (yf_ref, og_ref, mod_ref, wo_ref, gpm_ref, gpf_ref, gqf_ref,
     wup_ref, cw_ref, cb_ref, wdn_ref, out_ref, x1_s, a_s) = refs
    i = pl.program_id(0)
    mod = mod_ref[pl.ds(_mod_row(i, tm, seq_len, per_batch), 1), :]

    x = x_ref[...]
    if has_pe:
        x = _add_pos(x, pr_ref, pc_ref)
    y = _dot(yf_ref[...], wo_ref[0:FOURIER_W, :]) + _dot(og_ref[...], wo_ref[FOURIER_W:, :])
    x1 = x + _rms(y, gpm_ref[...] * mod[:, 2 * D_MODEL:3 * D_MODEL])
    x1_s[...] = x1
    shift_f = mod[:, 3 * D_MODEL:4 * D_MODEL]
    gain_f = gpf_ref[...] * (1.0 + mod[:, 4 * D_MODEL:5 * D_MODEL])
    h = (_rms(x1, gain_f) + shift_f).astype(BF16)

    pos = lax.broadcasted_iota(jnp.int32, (tm, 1), 0) % seg
    first = pos == 0
    last = pos == seg - 1

    def conv_cols(c0, width):
        cols = slice(c0, c0 + width)
        u = _dot(h, wup_ref[:, cols])
        u_prev = jnp.where(first, 0.0, pltpu.roll(u, 1, 0))
        u_next = jnp.where(last, 0.0, pltpu.roll(u, tm - 1, 0))
        return (u_prev * cw_ref[0:1, cols] + u * cw_ref[1:2, cols]
                + u_next * cw_ref[2:3, cols] + cb_ref[:, cols])

    for c0 in range(0, D_FF, FF_CHUNK):
        width = min(FF_CHUNK, D_FF - c0)
        val = conv_cols(c0, width)
        gate = conv_cols(D_FF + c0, width)
        a_s[:, c0:c0 + width] = (_silu(gate) * val).astype(BF16)

    y2 = _dot(a_s[...], wdn_ref[...])
    out_ref[...] = x1_s[...] + _rms(y2, gqf_ref[...] * mod[:, 5 * D_MODEL:6 * D_MODEL])


def _tail(x, pe, yf, og, mod_all, per_batch, layer, ffn_w, gpm, gpf, gqf, cw, cb, seq_len, seg, tm):
    n = x.shape[0]
    n_tiles = n // tm
    has_pe = pe is not None
    w_out, w_up, w_down = ffn_w
    row = lambda w: pl.BlockSpec((tm, w), lambda i: (i, 0))
    once = pl.Buffered(1)
    pick = lambda a, lead: pl.BlockSpec((None,) + a.shape[1:],
                                        lambda i: (lead,) + (0,) * (a.ndim - 1), pipeline_mode=once)
    lay = lambda a: pick(a, layer)
    own = lambda a: pick(a, 0)
    in_specs = [row(D_MODEL)]
    args = [x]
    if has_pe:
        pr, pc = pe
        in_specs += [pl.BlockSpec((tm // GRID_W, pr.shape[1]), lambda i: (i % (seq_len // tm), 0)),
                     pl.BlockSpec(pc.shape, lambda i: (0, 0), pipeline_mode=once)]
        args += [pr, pc]
    in_specs += [row(FOURIER_W), row(GLA_DV_W), lay(mod_all), own(w_out), lay(gpm), lay(gpf),
                 lay(gqf), own(w_up), lay(cw), lay(cb), own(w_down)]
    args += [yf, og, mod_all, w_out, gpm, gpf, gqf, w_up, cw, cb, w_down]
    return pl.pallas_call(
        functools.partial(_tail_kernel, has_pe=has_pe, per_batch=per_batch, seg=seg, tm=tm,
                          seq_len=seq_len),
        grid=(n_tiles,),
        in_specs=in_specs,
        out_specs=row(D_MODEL),
        out_shape=jax.ShapeDtypeStruct((n, D_MODEL), F32),
        scratch_shapes=[
            pltpu.VMEM((tm, D_MODEL), F32),
            pltpu.VMEM((tm, D_FF), BF16),
        ],
        compiler_params=pltpu.CompilerParams(
            dimension_semantics=("arbitrary",), vmem_limit_bytes=VMEM_LIMIT),
        name="tail",
    )(*args)


def _grid_pos_embed(rows, d):
    quarter = d // 4
    omega = 1.0 / (POS_BASE ** (jnp.arange(quarter, dtype=F32) / quarter))
    er = jnp.arange(rows, dtype=F32)[:, None] * omega
    ec = jnp.arange(GRID_W, dtype=F32)[:, None] * omega
    pr = jnp.concatenate([jnp.sin(er), jnp.cos(er)], axis=-1)
    pc = jnp.concatenate([jnp.sin(ec), jnp.cos(ec)], axis=-1)
    return pr, pc


def _dft_tables(n, scale):
    idx = jnp.arange(n, dtype=jnp.int32)
    ang = ((idx[:, None] * idx[None, :]) % n).astype(F32) * np.float32(2.0 * np.pi / n)
    return jnp.cos(ang) * np.float32(scale), jnp.sin(ang) * np.float32(scale)


def _seqdft_tables(seq_len):
    p = _dft_len(seq_len)
    radix = seq_len // p
    c, s = _dft_tables(p, seq_len ** -0.5)
    tables = (c.astype(BF16), (-s).astype(BF16))
    if radix > 1:
        k = (jnp.arange(p, dtype=jnp.int32)[:, None, None]
             + p * jnp.arange(radix, dtype=jnp.int32)[None, :, None])
        r = jnp.arange(radix, dtype=jnp.int32)[None, None, :]
        ang = ((k * r) % seq_len).astype(F32) * np.float32(2.0 * np.pi / seq_len)
        ang = ang.reshape(p, radix * radix)
        tables += (jnp.cos(ang), jnp.sin(ang))
    return tables


class _StreamCfg(NamedTuple):
    batch: int
    seq_len: int
    seg: int
    tm: int
    nb_dft: int
    nb_gla: int


def _stream_config(batch, seq_len, seg):
    return _StreamCfg(batch, seq_len, seg, tm=512,
                      nb_dft=max(1, min(batch, 1024 // seq_len)),
                      nb_gla=max(1, min(batch, 1024 // seq_len)))


def _stream_layer(x, pe, mod_all, per_batch, state_in, state_out, layer, p, ffn_w, dft, cfg):
    cc, seq_tables = dft
    batch, seq_len = cfg.batch, cfg.seq_len
    cast = (p['w_out'], p['w_up'], p['w_down']) if ffn_w is None else ()
    (ab, q, k, v, laf, lab, sg), cast_out = _inproj(
        x, pe, mod_all, per_batch, layer, p['g_pre_mix'], p['w_in'], p['w_gate_f'], p['b_gate_f'],
        p['w_gate_b'], p['b_gate_b'], cc, seq_len, cfg.tm, cast)
    if ffn_w is None:
        ffn_w = cast_out
    yf = _seqdft(ab, seq_tables, batch, seq_len, cfg.nb_dft)
    og, s_t = _gla(q, k, v, laf, lab, sg, p['g_gla'], state_in, state_out, layer, batch, seq_len,
                   cfg.nb_gla)
    x_new = _tail(x, pe, yf, og.reshape(batch * seq_len, GLA_DV_W), mod_all, per_batch, layer,
                  ffn_w, p['g_post_mix'], p['g_pre_ffn'], p['g_post_ffn'], p['conv_w'],
                  p['conv_b'], seq_len, cfg.seg, cfg.tm)
    return x_new, s_t, ffn_w


def kernel(x_prompt, x_sample, state_gla, c, c_ctx, g_pre_mix, g_post_mix, g_pre_ffn, g_post_ffn,
           w_ada, b_ada, w_in, w_gate_f, b_gate_f, w_gate_b, b_gate_b, g_gla, w_out, w_up, conv_w,
           conv_b, w_down):
    pb, pt, _ = x_prompt.shape
    sb, s_t, _ = x_sample.shape
    rows = s_t // GRID_W

    cv = jnp.concatenate([c_ctx[None], c, jnp.zeros((MOD_ROWS - 1 - sb, D_MODEL), F32)], axis=0)
    mod_all = _ada(cv, w_ada, b_ada)

    cc_c, cc_s = _dft_tables(FOURIER_HD, FOURIER_HD ** -0.5)
    cc = jnp.concatenate([cc_c, cc_s], axis=1).astype(BF16)
    dft_p = (cc, _seqdft_tables(pt))
    dft_s = (cc, _seqdft_tables(s_t))
    pe = _grid_pos_embed(rows, D_MODEL)

    xp = x_prompt.reshape(pb * pt, D_MODEL)
    xs = x_sample.reshape(sb * s_t, D_MODEL)
    new_state = jnp.zeros((pb, DEPTH, 2, GLA_HEADS, DK, DV), F32)
    vec = lambda a: a[:, None]
    p = dict(
        g_pre_mix=vec(g_pre_mix), g_post_mix=vec(g_post_mix), g_pre_ffn=vec(g_pre_ffn),
        g_post_ffn=vec(g_post_ffn), g_gla=vec(g_gla), w_in=w_in, w_gate_f=w_gate_f,
        b_gate_f=vec(b_gate_f), w_gate_b=w_gate_b, b_gate_b=vec(b_gate_b),
        w_out=w_out, w_up=w_up, conv_w=conv_w, conv_b=vec(conv_b), w_down=w_down)
    cfg_p = _stream_config(pb, pt, pt)
    cfg_s = _stream_config(sb, s_t, GRID_W)
    for l in range(DEPTH):
        xp, new_state, ffn_w = _stream_layer(xp, None, mod_all, False, None, new_state, l, p,
                                             None, dft_p, cfg_p)
        xs, _, _ = _stream_layer(xs, pe if l == 0 else None, mod_all, True, state_gla, None, l, p,
                                 ffn_w, dft_s, cfg_s)
    return (xp.reshape(pb, pt, D_MODEL), xs.reshape(sb, s_t, D_MODEL), new_state)
```

```python
import functools
from typing import NamedTuple

import numpy as np
import jax
import jax.numpy as jnp
from jax import lax
from jax.experimental import pallas as pl
from jax.experimental.pallas import tpu as pltpu

D_MODEL = 1024
DEPTH = 2
GRID_W = 64
FOURIER_W = 512
FOURIER_HEADS = 4
FOURIER_HD = FOURIER_W // FOURIER_HEADS
GLA_DV_W = 512
GLA_DK_W = 256
GLA_HEADS = 4
DV = GLA_DV_W // GLA_HEADS
DK = GLA_DK_W // GLA_HEADS
GATE_RANK = 16
GATE_TEMP = 16.0
CHUNK = 64
D_FF = 11 * D_MODEL // 4
EPS = 1e-6
POS_BASE = 10000.0
S_F = FOURIER_W
S_Q = S_F + GLA_DK_W
S_K = S_Q + GLA_DK_W
S_V = S_K + GLA_DV_W
S_AF = S_V + GATE_RANK
S_AB = S_AF + GATE_RANK
IN_COLS = S_AB + GLA_DV_W

LANE = 128
C_A = 0
C_F = C_A + LANE
C_Q = C_F + FOURIER_W
C_K = C_Q + GLA_DK_W
C_V = C_K + GLA_DK_W
C_G = C_V + GLA_DV_W
IN_COLS_PAD = C_G + GLA_DV_W
MOD_ROWS = 8
BLK = 256

VMEM_LIMIT = 56 * 1024 * 1024

F32 = jnp.float32
BF16 = jnp.bfloat16


def _silu(x):
    return x * jax.nn.sigmoid(x)


def _rms(x, g):
    return x * lax.rsqrt(jnp.mean(x * x, axis=-1, keepdims=True) + EPS) * g


def _dot(a, b):
    return jnp.dot(a, b, preferred_element_type=F32)


def _dot_nt(a, b):
    return lax.dot_general(a, b, (((1,), (1,)), ((), ())), preferred_element_type=F32)


def _add_pos(x, pr_ref, pc_ref):
    half = D_MODEL // 2
    reps = x.shape[0] // GRID_W
    row_part = jnp.concatenate(
        [jnp.broadcast_to(pr_ref[j:j + 1, :], (GRID_W, half)) for j in range(reps)], axis=0)
    col_part = jnp.concatenate([pc_ref[...]] * reps, axis=0)
    return x + jnp.concatenate([row_part, col_part], axis=1)


def _ada_kernel(cv_ref, w_ref, b_ref, o_ref):
    s = _silu(cv_ref[...]).astype(BF16)
    o_ref[0] = _dot(s, w_ref[0].astype(BF16)) + b_ref[0]


def _ada(cv, w_ada, b_ada):
    tn = 1536
    n_out = w_ada.shape[-1]
    return pl.pallas_call(
        _ada_kernel,
        grid=(DEPTH, n_out // tn),
        in_specs=[
            pl.BlockSpec((MOD_ROWS, D_MODEL), lambda l, n: (0, 0)),
            pl.BlockSpec((1, D_MODEL, tn), lambda l, n: (l, 0, n)),
            pl.BlockSpec((1, 1, tn), lambda l, n: (l, 0, n)),
        ],
        out_specs=pl.BlockSpec((1, MOD_ROWS, tn), lambda l, n: (l, 0, n)),
        out_shape=jax.ShapeDtypeStruct((DEPTH, MOD_ROWS, n_out), F32),
        compiler_params=pltpu.CompilerParams(
            dimension_semantics=("arbitrary", "arbitrary"), vmem_limit_bytes=VMEM_LIMIT),
        name="ada",
    )(cv, w_ada, b_ada.reshape(DEPTH, 1, n_out))


def _mod_row(tile, tm, seq_len, per_batch):
    return 1 + (tile * tm) // seq_len if per_batch else 0


def _inproj_kernel(*refs, has_pe, per_batch, radix, tm, seq_len, n_cast):
    if has_pe:
        x_ref, pr_ref, pc_ref, *refs = refs
    else:
        x_ref, *refs = refs
    if radix > 1:
        *refs, ab_s = refs
    mod_ref, g_ref, w_ref, wgf_ref, wgb_ref, bgf_ref, bgb_ref, cc_ref, *refs = refs
    cast_in, refs = refs[:n_cast], refs[n_cast:]
    ab_ref, q_ref, k_ref, v_ref, laf_ref, lab_ref, sg_ref, *refs = refs
    cast_out, (w_s, wg_s) = refs[:n_cast], refs[n_cast:]
    i = pl.program_id(0)

    def cast_weights():
        rows_per = 256
        for r0 in range(0, D_MODEL, rows_per):
            rs = slice(r0, r0 + rows_per)
            w_s[rs, C_A:C_F] = jnp.zeros((rows_per, LANE), BF16)
            w_s[rs, C_A:C_A + 2 * GATE_RANK] = w_ref[rs, S_V:S_AB].astype(BF16)
            w_s[rs, C_F:C_G] = w_ref[rs, 0:S_V].astype(BF16)
            w_s[rs, C_G:IN_COLS_PAD] = w_ref[rs, S_AB:IN_COLS].astype(BF16)
        wg_s[...] = jnp.zeros_like(wg_s)
        wg_s[0:GATE_RANK, 0:GLA_DK_W] = wgf_ref[...].astype(BF16)
        wg_s[GATE_RANK:2 * GATE_RANK, GLA_DK_W:] = wgb_ref[...].astype(BF16)

    def norm_tile():
        x = x_ref[...]
        if has_pe:
            x = _add_pos(x, pr_ref, pc_ref)
        mod = mod_ref[pl.ds(_mod_row(i, tm, seq_len, per_batch), 1), :]
        shift = mod[:, 0:D_MODEL]
        gain = g_ref[...] * (1.0 + mod[:, D_MODEL:2 * D_MODEL])
        return (_rms(x, gain) + shift).astype(BF16)

    def project_tile(h):
        z_af = _dot(h, w_s[:, C_A:C_Q])
        z_qk = _dot(h, w_s[:, C_Q:C_V])
        xg = _dot(z_af[:, C_A:C_F].astype(BF16), wg_s[...])
        zf = z_af[:, C_F:C_Q].astype(BF16)
        for hh in range(FOURIER_HEADS):
            sl = slice(hh * FOURIER_HD, (hh + 1) * FOURIER_HD)
            cs = _dot(zf[:, sl], cc_ref[...])
            if radix == 1:
                ab_ref[0, :, sl] = cs[:, :FOURIER_HD].astype(BF16)
                ab_ref[1, :, sl] = cs[:, FOURIER_HD:].astype(BF16)
            else:
                ab_s[0, hh] = cs[:, :FOURIER_HD]
                ab_s[1, hh] = cs[:, FOURIER_HD:]
        z_vg = _dot(h, w_s[:, C_V:IN_COLS_PAD])
        for ref, bias, cols in ((laf_ref, bgf_ref, slice(0, GLA_DK_W)),
                                (lab_ref, bgb_ref, slice(GLA_DK_W, 2 * GLA_DK_W))):
            xb = xg[:, cols] + bias[...]
            ref[...] = (jnp.minimum(xb, 0.0) - jnp.log1p(jnp.exp(-jnp.abs(xb)))) * (1.0 / GATE_TEMP)
        q_ref[...] = z_qk[:, 0:GLA_DK_W] * (DK ** -0.5)
        k_ref[...] = z_qk[:, GLA_DK_W:]
        if radix > 1:
            for r in range(radix):
                for part in range(2):
                    for hh in range(FOURIER_HEADS):
                        c0 = r * FOURIER_W + hh * FOURIER_HD
                        ab_ref[part, :, c0:c0 + FOURIER_HD] = (
                            ab_s[part, hh, pl.ds(r, tm // radix, stride=radix), :].astype(BF16))
        v_ref[...] = z_vg[:, 0:GLA_DV_W].astype(BF16)
        sg_ref[...] = _silu(z_vg[:, GLA_DV_W:]).astype(BF16)

    pl.when(i == 0)(cast_weights)
    project_tile(norm_tile())
    for src, dst in zip(cast_in, cast_out):
        dst[...] = src[...].astype(BF16)


def _inproj(x, pe, mod_all, per_batch, layer, g, w_in, w_gate_f, b_gate_f, w_gate_b, b_gate_b, cc,
            seq_len, tm, cast=()):
    n = x.shape[0]
    n_tiles = n // tm
    radix = seq_len // _dft_len(seq_len)
    has_pe = pe is not None
    once = pl.Buffered(1)
    lay = lambda a: pl.BlockSpec((None,) + a.shape[1:], lambda i: (layer,) + (0,) * (a.ndim - 1),
                                 pipeline_mode=once)
    row = lambda w: pl.BlockSpec((tm, w), lambda i: (i, 0))
    in_specs = [row(D_MODEL)]
    args = [x]
    if has_pe:
        pr, pc = pe
        in_specs += [pl.BlockSpec((tm // GRID_W, pr.shape[1]), lambda i: (i % (seq_len // tm), 0)),
                     pl.BlockSpec(pc.shape, lambda i: (0, 0), pipeline_mode=once)]
        args += [pr, pc]
    weights = [mod_all, g, w_in, w_gate_f, w_gate_b, b_gate_f, b_gate_b]
    in_specs += [lay(a) for a in weights] + [pl.BlockSpec(cc.shape, lambda i: (0, 0),
                                                          pipeline_mode=once)]
    args += weights + [cc]
    assert all(a.shape[1] % n_tiles == 0 for a in cast)
    cast_rows = [a.shape[1] // n_tiles for a in cast]
    in_specs += [pl.BlockSpec((None, r, a.shape[2]), lambda i: (layer, i, 0))
                 for a, r in zip(cast, cast_rows)]
    args += list(cast)
    out_shape = (
        jax.ShapeDtypeStruct((2, n // radix, radix * FOURIER_W), BF16),
        jax.ShapeDtypeStruct((n, GLA_DK_W), F32),
        jax.ShapeDtypeStruct((n, GLA_DK_W), F32),
        jax.ShapeDtypeStruct((n, GLA_DV_W), BF16),
        jax.ShapeDtypeStruct((n, GLA_DK_W), F32),
        jax.ShapeDtypeStruct((n, GLA_DK_W), F32),
        jax.ShapeDtypeStruct((n, GLA_DV_W), BF16),
    )
    out_specs = (
        pl.BlockSpec((2, tm // radix, radix * FOURIER_W), lambda i: (0, i, 0)),
        row(GLA_DK_W), row(GLA_DK_W), row(GLA_DV_W), row(GLA_DK_W), row(GLA_DK_W), row(GLA_DV_W),
    )
    out_shape += tuple(jax.ShapeDtypeStruct((1,) + a.shape[1:], BF16) for a in cast)
    out_specs += tuple(pl.BlockSpec((None, r, a.shape[2]), lambda i: (0, i, 0))
                       for a, r in zip(cast, cast_rows))
    scratch = [pltpu.VMEM((D_MODEL, IN_COLS_PAD), BF16), pltpu.VMEM((LANE, 2 * GLA_DK_W), BF16)]
    if radix > 1:
        scratch.append(pltpu.VMEM((2, FOURIER_HEADS, tm, FOURIER_HD), F32))
    outs = pl.pallas_call(
        functools.partial(_inproj_kernel, has_pe=has_pe, per_batch=per_batch, radix=radix, tm=tm,
                          seq_len=seq_len, n_cast=len(cast)),
        grid=(n_tiles,),
        in_specs=in_specs,
        out_specs=out_specs,
        out_shape=out_shape,
        scratch_shapes=scratch,
        compiler_params=pltpu.CompilerParams(
            dimension_semantics=("arbitrary",), vmem_limit_bytes=VMEM_LIMIT),
        name="inproj",
    )(*args)
    return outs[:7], outs[7:]


DFT_MAX = 512


def _dft_len(seq_len):
    return min(seq_len, DFT_MAX)


def _seqdft_kernel(*refs, radix, nb):
    if radix > 1:
        c_ref, sn_ref, twc_ref, tws_ref, a_ref, b_ref, o_ref = refs
    else:
        c_ref, sn_ref, a_ref, b_ref, o_ref = refs
    c = c_ref[...]
    sn = sn_ref[...]
    for bi in range(nb):
        a = a_ref[bi]
        b = b_ref[bi]
        g_re = _dot(c, a) + _dot(sn, b)
        if radix == 1:
            o_ref[bi, 0] = g_re.astype(BF16)
            continue
        g_im = _dot(sn, a) - _dot(c, b)
        for k2 in range(radix):
            acc = g_re[:, 0:FOURIER_W]
            for r in range(1, radix):
                j = k2 * radix + r
                cols = slice(r * FOURIER_W, (r + 1) * FOURIER_W)
                acc = acc + twc_ref[:, j:j + 1] * g_re[:, cols] + tws_ref[:, j:j + 1] * g_im[:, cols]
            o_ref[bi, k2] = acc.astype(BF16)


def _seqdft(ab, tables, batch, seq_len, nb):
    p = _dft_len(seq_len)
    radix = seq_len // p
    width = radix * FOURIER_W
    ab4 = ab.reshape(2, batch, p, width)
    full = lambda a: pl.BlockSpec(a.shape, lambda i: (0,) * a.ndim)
    out = pl.pallas_call(
        functools.partial(_seqdft_kernel, radix=radix, nb=nb),
        grid=(batch // nb,),
        in_specs=[full(t) for t in tables] + [
            pl.BlockSpec((None, nb, p, width), lambda i: (0, i, 0, 0)),
            pl.BlockSpec((None, nb, p, width), lambda i: (1, i, 0, 0)),
        ],
        out_specs=pl.BlockSpec((nb, radix, p, FOURIER_W), lambda i: (i, 0, 0, 0)),
        out_shape=jax.ShapeDtypeStruct((batch, radix, p, FOURIER_W), BF16),
        compiler_params=pltpu.CompilerParams(
            dimension_semantics=("arbitrary",), vmem_limit_bytes=VMEM_LIMIT),
        name="seqdft",
    )(*tables, ab4, ab4)
    return out.reshape(batch * seq_len, FOURIER_W)


NEW_STATES = "new"


def _gla_kernel(*refs, seq_len, nb, has_s0, want_state, new_state_layer):
    if has_s0:
        s0_ref, *refs = refs
    if want_state and new_state_layer is None:
        _, *refs = refs
    q_ref, k_ref, v_ref, laf_ref, lab_ref, sg_ref, g_ref, og_ref, *refs = refs
    if want_state:
        sT_ref, *refs = refs
    qt_s, kt_s, ke_s, dec_s, o_s, st_s = refs
    nc = seq_len // CHUNK
    la_refs = (laf_ref, lab_ref)
    chains = [(bi, d) for bi in range(nb) for d in range(2)]

    ri = lax.broadcasted_iota(jnp.int32, (BLK, BLK), 0)
    ci = lax.broadcasted_iota(jnp.int32, (BLK, BLK), 1)
    same = (ri // CHUNK) == (ci // CHUNK)
    tri = (jnp.where(same & (ci <= ri), 1.0, 0.0).astype(BF16),
           jnp.where(same & (ci >= ri), 1.0, 0.0).astype(BF16))

    for bi, blk in [(bi, blk) for bi in range(nb) for blk in range(seq_len // BLK)]:
        rows = slice(blk * BLK, (blk + 1) * BLK)
        q = q_ref[bi, rows, :]
        k = k_ref[bi, rows, :]
        for d in range(2):
            la = la_refs[d][bi, rows, :]
            hi = la.astype(BF16)
            lo = (la - hi.astype(F32)).astype(BF16)
            b = _dot(tri[d], hi) + _dot(tri[d], lo)
            edge = CHUNK - 1 if d == 0 else 0
            tot = jnp.concatenate(
                [jnp.broadcast_to(b[cc * CHUNK + edge:cc * CHUNK + edge + 1, :], (CHUNK, GLA_DK_W))
                 for cc in range(BLK // CHUNK)], axis=0)
            qt_s[bi, d, rows, :] = (q * jnp.exp(b)).astype(BF16)
            kt_s[bi, d, rows, :] = (k * jnp.exp(-b)).astype(BF16)
            ke_s[bi, d, rows, :] = (k * jnp.exp(tot - b)).astype(BF16)
            for cc in range(BLK // CHUNK):
                c = blk * (BLK // CHUNK) + cc
                dec_s[bi, d, c:c + 1, :] = jnp.exp(tot[cc * CHUNK:cc * CHUNK + 1, :])
    for bi, d in chains:
        if has_s0:
            st_s[bi, d] = jnp.concatenate(
                [s0_ref[bi, d, h] for h in range(GLA_HEADS)], axis=0).T
        else:
            st_s[bi, d] = jnp.zeros((DV, GLA_DK_W), F32)

    lane_head = lax.broadcasted_iota(jnp.int32, (1, GLA_DK_W), 1) // DK
    head_mask = [jnp.where(lane_head == h, 1.0, 0.0).astype(BF16) for h in range(GLA_HEADS)]
    ti = lax.broadcasted_iota(jnp.int32, (CHUNK, CHUNK), 0)
    tj = lax.broadcasted_iota(jnp.int32, (CHUNK, CHUNK), 1)
    causal = (tj <= ti, tj >= ti)

    def chunk_rows(c):
        r0 = c * CHUNK
        return pl.ds(r0 if isinstance(r0, int) else pl.multiple_of(r0, CHUNK), CHUNK)

    def scan_step(step):
        chunk = (step, nc - 1 - step)
        rows = [chunk_rows(c) for c in chunk]
        v_c = {bi: [v_ref[bi, rows[d], :] for d in range(2)] for bi in range(nb)}
        res = {}
        for bi, d in chains:
            qt_c = qt_s[bi, d, rows[d], :]
            q_stack = jnp.concatenate([qt_c * head_mask[h] for h in range(GLA_HEADS)], axis=0)
            w = jnp.concatenate([st_s[bi, d].astype(BF16), kt_s[bi, d, rows[d], :]], axis=0)
            res[bi, d] = _dot_nt(q_stack, w)
        kv = {}
        for bi, d in chains:
            ke_c = ke_s[bi, d, rows[d], :]
            k_bd = jnp.concatenate([ke_c * head_mask[h] for h in range(GLA_HEADS)], axis=0)
            v_stack = jnp.concatenate(
                [v_c[bi][d][:, h * DV:(h + 1) * DV] for h in range(GLA_HEADS)], axis=0)
            v_stack_t = v_stack.astype(F32).T.astype(BF16)
            kv[bi, d] = _dot(v_stack_t, k_bd)
        for bi, d in chains:
            st_s[bi, d] = dec_s[bi, d, pl.ds(chunk[d], 1), :] * st_s[bi, d] + kv[bi, d]
        for bi, d in chains:
            for h in range(GLA_HEADS):
                hr = slice(h * CHUNK, (h + 1) * CHUNK)
                hv = slice(h * DV, (h + 1) * DV)
                att = jnp.where(causal[d], res[bi, d][hr, DV:DV + CHUNK], 0.0).astype(BF16)
                o_s[bi, d, rows[d], hv] = res[bi, d][hr, 0:DV] + _dot(att, v_c[bi][d][:, hv])

    if nc <= 4:
        for i in range(nc):
            scan_step(i)
    else:
        def body(i, carry):
            scan_step(i)
            return carry
        lax.fori_loop(0, nc, body, 0, unroll=4)

    g = g_ref[...]
    for bi in range(nb):
        for h in range(GLA_HEADS):
            hv = slice(h * DV, (h + 1) * DV)
            o = o_s[bi, 0, :, hv] + o_s[bi, 1, :, hv]
            og_ref[bi, :, hv] = (_rms(o, g) * sg_ref[bi, :, hv].astype(F32)).astype(BF16)
    if want_state:
        if new_state_layer is not None:
            own = sT_ref.at[:, new_state_layer]
            for l in range(DEPTH):
                if l != new_state_layer:
                    sT_ref[:, l] = jnp.zeros((nb, 2, GLA_HEADS, DK, DV), F32)
        else:
            own = sT_ref
        for bi, d in chains:
            s_d = st_s[bi, d].T
            for h in range(GLA_HEADS):
                own[bi, d, h] = s_d[h * DK:(h + 1) * DK, :]


def _gla(q, k, v, laf, lab, sg, g_gla, state_in, state_out, layer, batch, seq_len, nb):
    has_s0 = state_in is not None
    want_state = state_out is not None
    create = state_out is NEW_STATES
    r3 = lambda a: a.reshape(batch, seq_len, a.shape[-1])
    seq = lambda w: pl.BlockSpec((nb, seq_len, w), lambda b: (b, 0, 0))
    st_spec = pl.BlockSpec((nb, None, 2, GLA_HEADS, DK, DV), lambda b: (b, layer, 0, 0, 0, 0))
    in_specs = []
    args = []
    if has_s0:
        in_specs.append(st_spec)
        args.append(state_in)
    if want_state and not create:
        in_specs.append(pl.BlockSpec(memory_space=pl.ANY))
        args.append(state_out)
    in_specs += [seq(GLA_DK_W), seq(GLA_DK_W), seq(GLA_DV_W), seq(GLA_DK_W), seq(GLA_DK_W),
                 seq(GLA_DV_W), pl.BlockSpec((None, 1, DV), lambda b: (layer, 0, 0))]
    args += [r3(q), r3(k), r3(v), r3(laf), r3(lab), r3(sg), g_gla]
    out_specs = [seq(GLA_DV_W)]
    out_shape = [jax.ShapeDtypeStruct((batch, seq_len, GLA_DV_W), BF16)]
    if want_state:
        out_specs.append(
            pl.BlockSpec((nb, DEPTH, 2, GLA_HEADS, DK, DV), lambda b: (b, 0, 0, 0, 0, 0))
            if create else st_spec)
        out_shape.append(jax.ShapeDtypeStruct((batch, DEPTH, 2, GLA_HEADS, DK, DV), F32))
    res = pl.pallas_call(
        functools.partial(_gla_kernel, seq_len=seq_len, nb=nb, has_s0=has_s0,
                          want_state=want_state, new_state_layer=layer if create else None),
        grid=(batch // nb,),
        in_specs=in_specs,
        out_specs=out_specs,
        out_shape=out_shape,
        input_output_aliases={int(has_s0): 1} if want_state and not create else {},
        scratch_shapes=[
            pltpu.VMEM((nb, 2, seq_len, GLA_DK_W), BF16),
            pltpu.VMEM((nb, 2, seq_len, GLA_DK_W), BF16),
            pltpu.VMEM((nb, 2, seq_len, GLA_DK_W), BF16),
            pltpu.VMEM((nb, 2, seq_len // CHUNK, GLA_DK_W), F32),
            pltpu.VMEM((nb, 2, seq_len, GLA_DV_W), F32),
            pltpu.VMEM((nb, 2, DV, GLA_DK_W), F32),
        ],
        compiler_params=pltpu.CompilerParams(
            dimension_semantics=("arbitrary",), vmem_limit_bytes=VMEM_LIMIT),
        name="gla",
    )(*args)
    return res[0], (res[1] if want_state else None)


FF_CHUNK = 256


def _tail_kernel(*refs, has_pe, per_batch, seg, tm, seq_len):
    if has_pe:
        x_ref, pr_ref, pc_ref, *refs = refs
    else:
        x_ref, *refs = refs
    (yf_ref, og_ref, mod_ref, wo_ref, gpm_ref, gpf_ref, gqf_ref,
     wup_ref, cw_ref, cb_ref, wdn_ref, out_ref, x1_s, a_s) = refs
    i = pl.program_id(0)
    mod = mod_ref[pl.ds(_mod_row(i, tm, seq_len, per_batch), 1), :]

    x = x_ref[...]
    if has_pe:
        x = _add_pos(x, pr_ref, pc_ref)
    y = _dot(yf_ref[...], wo_ref[0:FOURIER_W, :]) + _dot(og_ref[...], wo_ref[FOURIER_W:, :])
    x1 = x + _rms(y, gpm_ref[...] * mod[:, 2 * D_MODEL:3 * D_MODEL])
    x1_s[...] = x1
    shift_f = mod[:, 3 * D_MODEL:4 * D_MODEL]
    gain_f = gpf_ref[...] * (1.0 + mod[:, 4 * D_MODEL:5 * D_MODEL])
    h = (_rms(x1, gain_f) + shift_f).astype(BF16)

    pos = lax.broadcasted_iota(jnp.int32, (tm, 1), 0) % seg
    first = pos == 0
    last = pos == seg - 1

    def conv_cols(c0, width):
        cols = slice(c0, c0 + width)
        u = _dot(h, wup_ref[:, cols])
        u_prev = jnp.where(first, 0.0, pltpu.roll(u, 1, 0))
        u_next = jnp.where(last, 0.0, pltpu.roll(u, tm - 1, 0))
        return (u_prev * cw_ref[0:1, cols] + u * cw_ref[1:2, cols]
                + u_next * cw_ref[2:3, cols] + cb_ref[:, cols])

    for c0 in range(0, D_FF, FF_CHUNK):
        width = min(FF_CHUNK, D_FF - c0)
        val = conv_cols(c0, width)
        gate = conv_cols(D_FF + c0, width)
        a_s[:, c0:c0 + width] = (_silu(gate) * val).astype(BF16)

    y2 = _dot(a_s[...], wdn_ref[...])
    out_ref[...] = x1_s[...] + _rms(y2, gqf_ref[...] * mod[:, 5 * D_MODEL:6 * D_MODEL])


def _tail(x, pe, yf, og, mod_all, per_batch, layer, ffn_w, gpm, gpf, gqf, cw, cb, seq_len, seg, tm):
    n = x.shape[0]
    n_tiles = n // tm
    has_pe = pe is not None
    w_out, w_up, w_down = ffn_w
    row = lambda w: pl.BlockSpec((tm, w), lambda i: (i, 0))
    once = pl.Buffered(1)
    pick = lambda a, lead: pl.BlockSpec((None,) + a.shape[1:],
                                        lambda i: (lead,) + (0,) * (a.ndim - 1), pipeline_mode=once)
    lay = lambda a: pick(a, layer)
    own = lambda a: pick(a, 0)
    in_specs = [row(D_MODEL)]
    args = [x]
    if has_pe:
        pr, pc = pe
        in_specs += [pl.BlockSpec((tm // GRID_W, pr.shape[1]), lambda i: (i % (seq_len // tm), 0)),
                     pl.BlockSpec(pc.shape, lambda i: (0, 0), pipeline_mode=once)]
        args += [pr, pc]
    in_specs += [row(FOURIER_W), row(GLA_DV_W), lay(mod_all), own(w_out), lay(gpm), lay(gpf),
                 lay(gqf), own(w_up), lay(cw), lay(cb), own(w_down)]
    args += [yf, og, mod_all, w_out, gpm, gpf, gqf, w_up, cw, cb, w_down]
    return pl.pallas_call(
        functools.partial(_tail_kernel, has_pe=has_pe, per_batch=per_batch, seg=seg, tm=tm,
                          seq_len=seq_len),
        grid=(n_tiles,),
        in_specs=in_specs,
        out_specs=row(D_MODEL),
        out_shape=jax.ShapeDtypeStruct((n, D_MODEL), F32),
        scratch_shapes=[
            pltpu.VMEM((tm, D_MODEL), F32),
            pltpu.VMEM((tm, D_FF), BF16),
        ],
        compiler_params=pltpu.CompilerParams(
            dimension_semantics=("arbitrary",), vmem_limit_bytes=VMEM_LIMIT),
        name="tail",
    )(*args)


def _grid_pos_embed(rows, d):
    quarter = d // 4
    omega = 1.0 / (POS_BASE ** (jnp.arange(quarter, dtype=F32) / quarter))
    er = jnp.arange(rows, dtype=F32)[:, None] * omega
    ec = jnp.arange(GRID_W, dtype=F32)[:, None] * omega
    pr = jnp.concatenate([jnp.sin(er), jnp.cos(er)], axis=-1)
    pc = jnp.concatenate([jnp.sin(ec), jnp.cos(ec)], axis=-1)
    return pr, pc


def _dft_tables(n, scale):
    idx = jnp.arange(n, dtype=jnp.int32)
    ang = ((idx[:, None] * idx[None, :]) % n).astype(F32) * np.float32(2.0 * np.pi / n)
    return jnp.cos(ang) * np.float32(scale), jnp.sin(ang) * np.float32(scale)


def _seqdft_tables(seq_len):
    p = _dft_len(seq_len)
    radix = seq_len // p
    c, s = _dft_tables(p, seq_len ** -0.5)
    tables = (c.astype(BF16), (-s).astype(BF16))
    if radix > 1:
        k = (jnp.arange(p, dtype=jnp.int32)[:, None, None]
             + p * jnp.arange(radix, dtype=jnp.int32)[None, :, None])
        r = jnp.arange(radix, dtype=jnp.int32)[None, None, :]
        ang = ((k * r) % seq_len).astype(F32) * np.float32(2.0 * np.pi / seq_len)
        ang = ang.reshape(p, radix * radix)
        tables += (jnp.cos(ang), jnp.sin(ang))
    return tables


class _StreamCfg(NamedTuple):
    batch: int
    seq_len: int
    seg: int
    tm: int
    nb_dft: int
    nb_gla: int


def _stream_config(batch, seq_len, seg):
    return _StreamCfg(batch, seq_len, seg, tm=512,
                      nb_dft=max(1, min(batch, 1024 // seq_len)),
                      nb_gla=max(1, min(batch, 1024 // seq_len)))


def _stream_layer(x, pe, mod_all, per_batch, state_in, state_out, layer, p, ffn_w, dft, cfg):
    cc, seq_tables = dft
    batch, seq_len = cfg.batch, cfg.seq_len
    cast = (p['w_out'], p['w_up'], p['w_down']) if ffn_w is None else ()
    (ab, q, k, v, laf, lab, sg), cast_out = _inproj(
        x, pe, mod_all, per_batch, layer, p['g_pre_mix'], p['w_in'], p['w_gate_f'], p['b_gate_f'],
        p['w_gate_b'], p['b_gate_b'], cc, seq_len, cfg.tm, cast)
    if ffn_w is None:
        ffn_w = cast_out
    yf = _seqdft(ab, seq_tables, batch, seq_len, cfg.nb_dft)
    og, s_t = _gla(q, k, v, laf, lab, sg, p['g_gla'], state_in, state_out, layer, batch, seq_len,
                   cfg.nb_gla)
    x_new = _tail(x, pe, yf, og.reshape(batch * seq_len, GLA_DV_W), mod_all, per_batch, layer,
                  ffn_w, p['g_post_mix'], p['g_pre_ffn'], p['g_post_ffn'], p['conv_w'],
                  p['conv_b'], seq_len, cfg.seg, cfg.tm)
    return x_new, s_t, ffn_w


def kernel(x_prompt, x_sample, state_gla, c, c_ctx, g_pre_mix, g_post_mix, g_pre_ffn, g_post_ffn,
           w_ada, b_ada, w_in, w_gate_f, b_gate_f, w_gate_b, b_gate_b, g_gla, w_out, w_up, conv_w,
           conv_b, w_down):
    pb, pt, _ = x_prompt.shape
    sb, s_t, _ = x_sample.shape
    rows = s_t // GRID_W

    cv = jnp.concatenate([c_ctx[None], c, jnp.zeros((MOD_ROWS - 1 - sb, D_MODEL), F32)], axis=0)
    mod_all = _ada(cv, w_ada, b_ada)

    cc_c, cc_s = _dft_tables(FOURIER_HD, FOURIER_HD ** -0.5)
    cc = jnp.concatenate([cc_c, cc_s], axis=1).astype(BF16)
    dft_p = (cc, _seqdft_tables(pt))
    dft_s = (cc, _seqdft_tables(s_t))
    pe = _grid_pos_embed(rows, D_MODEL)

    xp = x_prompt.reshape(pb * pt, D_MODEL)
    xs = x_sample.reshape(sb * s_t, D_MODEL)
    new_state = NEW_STATES
    vec = lambda a: a[:, None]
    p = dict(
        g_pre_mix=vec(g_pre_mix), g_post_mix=vec(g_post_mix), g_pre_ffn=vec(g_pre_ffn),
        g_post_ffn=vec(g_post_ffn), g_gla=vec(g_gla), w_in=w_in, w_gate_f=w_gate_f,
        b_gate_f=vec(b_gate_f), w_gate_b=w_gate_b, b_gate_b=vec(b_gate_b),
        w_out=w_out, w_up=w_up, conv_w=conv_w, conv_b=vec(conv_b), w_down=w_down)
    cfg_p = _stream_config(pb, pt, pt)
    cfg_s = _stream_config(sb, s_t, GRID_W)
    for l in range(DEPTH):
        xp, new_state, ffn_w = _stream_layer(xp, None, mod_all, False, None, new_state, l, p,
                                             None, dft_p, cfg_p)
        xs, _, _ = _stream_layer(xs, pe if l == 0 else None, mod_all, True, state_gla, None, l, p,
                                 ffn_w, dft_s, cfg_s)
    return (xp.reshape(pb, pt, D_MODEL), xs.reshape(sb, s_t, D_MODEL), new_state)
```

```python
import functools
from typing import NamedTuple

import numpy as np
import jax
import jax.numpy as jnp
from jax import lax
from jax.experimental import pallas as pl
from jax.experimental.pallas import tpu as pltpu

D_MODEL = 1024
DEPTH = 2
GRID_W = 64
FOURIER_W = 512
FOURIER_HEADS = 4
FOURIER_HD = FOURIER_W // FOURIER_HEADS
GLA_DV_W = 512
GLA_DK_W = 256
GLA_HEADS = 4
DV = GLA_DV_W // GLA_HEADS
DK = GLA_DK_W // GLA_HEADS
GATE_RANK = 16
GATE_TEMP = 16.0
CHUNK = 64
D_FF = 11 * D_MODEL // 4
EPS = 1e-6
POS_BASE = 10000.0
S_F = FOURIER_W
S_Q = S_F + GLA_DK_W
S_K = S_Q + GLA_DK_W
S_V = S_K + GLA_DV_W
S_AF = S_V + GATE_RANK
S_AB = S_AF + GATE_RANK
IN_COLS = S_AB + GLA_DV_W

LANE = 128
C_A = 0
C_F = C_A + LANE
C_Q = C_F + FOURIER_W
C_K = C_Q + GLA_DK_W
C_V = C_K + GLA_DK_W
C_G = C_V + GLA_DV_W
IN_COLS_PAD = C_G + GLA_DV_W
MOD_ROWS = 8
BLK = 256

VMEM_LIMIT = 56 * 1024 * 1024

F32 = jnp.float32
BF16 = jnp.bfloat16


def _silu(x):
    return x * jax.nn.sigmoid(x)


def _rms(x, g):
    return x * lax.rsqrt(jnp.mean(x * x, axis=-1, keepdims=True) + EPS) * g


def _dot(a, b):
    return jnp.dot(a, b, preferred_element_type=F32)


def _dot_nt(a, b):
    return lax.dot_general(a, b, (((1,), (1,)), ((), ())), preferred_element_type=F32)


def _add_pos(x, pr_ref, pc_ref):
    half = D_MODEL // 2
    reps = x.shape[0] // GRID_W
    row_part = jnp.concatenate(
        [jnp.broadcast_to(pr_ref[j:j + 1, :], (GRID_W, half)) for j in range(reps)], axis=0)
    col_part = jnp.concatenate([pc_ref[...]] * reps, axis=0)
    return x + jnp.concatenate([row_part, col_part], axis=1)


def _ada_kernel(cv_ref, w_ref, b_ref, o_ref):
    s = _silu(cv_ref[...]).astype(BF16)
    o_ref[0] = _dot(s, w_ref[0].astype(BF16)) + b_ref[0]


ADA_COLS = 1536


def _ada(cv, w_ada, b_ada):
    tn = ADA_COLS
    n_out = w_ada.shape[-1]
    return pl.pallas_call(
        _ada_kernel,
        grid=(DEPTH, n_out // tn),
        in_specs=[
            pl.BlockSpec((MOD_ROWS, D_MODEL), lambda l, n: (0, 0)),
            pl.BlockSpec((1, D_MODEL, tn), lambda l, n: (l, 0, n)),
            pl.BlockSpec((1, 1, tn), lambda l, n: (l, 0, n)),
        ],
        out_specs=pl.BlockSpec((1, MOD_ROWS, tn), lambda l, n: (l, 0, n)),
        out_shape=jax.ShapeDtypeStruct((DEPTH, MOD_ROWS, n_out), F32),
        compiler_params=pltpu.CompilerParams(
            dimension_semantics=("arbitrary", "arbitrary"), vmem_limit_bytes=VMEM_LIMIT),
        name="ada",
    )(cv, w_ada, b_ada.reshape(DEPTH, 1, n_out))


def _mod_row(tile, tm, seq_len, per_batch):
    return 1 + (tile * tm) // seq_len if per_batch else 0


def _inproj_kernel(*refs, has_pe, per_batch, radix, tm, seq_len, n_cast):
    if has_pe:
        x_ref, pr_ref, pc_ref, *refs = refs
    else:
        x_ref, *refs = refs
    if radix > 1:
        *refs, ab_s = refs
    mod_ref, g_ref, w_ref, wgf_ref, wgb_ref, bgf_ref, bgb_ref, cc_ref, *refs = refs
    cast_in, refs = refs[:n_cast], refs[n_cast:]
    ab_ref, q_ref, k_ref, v_ref, laf_ref, lab_ref, sg_ref, *refs = refs
    cast_out, (w_s, wg_s) = refs[:n_cast], refs[n_cast:]
    i = pl.program_id(0)

    def cast_weights():
        rows_per = BLK
        for r0 in range(0, D_MODEL, rows_per):
            rs = slice(r0, r0 + rows_per)
            w_s[rs, C_A:C_F] = jnp.zeros((rows_per, LANE), BF16)
            w_s[rs, C_A:C_A + 2 * GATE_RANK] = w_ref[rs, S_V:S_AB].astype(BF16)
            w_s[rs, C_F:C_G] = w_ref[rs, 0:S_V].astype(BF16)
            w_s[rs, C_G:IN_COLS_PAD] = w_ref[rs, S_AB:IN_COLS].astype(BF16)
        wg_s[...] = jnp.zeros_like(wg_s)
        wg_s[0:GATE_RANK, 0:GLA_DK_W] = wgf_ref[...].astype(BF16)
        wg_s[GATE_RANK:2 * GATE_RANK, GLA_DK_W:] = wgb_ref[...].astype(BF16)

    def norm_tile():
        x = x_ref[...]
        if has_pe:
            x = _add_pos(x, pr_ref, pc_ref)
        mod = mod_ref[pl.ds(_mod_row(i, tm, seq_len, per_batch), 1), :]
        shift = mod[:, 0:D_MODEL]
        gain = g_ref[...] * (1.0 + mod[:, D_MODEL:2 * D_MODEL])
        return (_rms(x, gain) + shift).astype(BF16)

    def project_tile(h):
        z_af = _dot(h, w_s[:, C_A:C_Q])
        z_qk = _dot(h, w_s[:, C_Q:C_V])
        xg = _dot(z_af[:, C_A:C_F].astype(BF16), wg_s[...])
        zf = z_af[:, C_F:C_Q].astype(BF16)
        for hh in range(FOURIER_HEADS):
            sl = slice(hh * FOURIER_HD, (hh + 1) * FOURIER_HD)
            cs = _dot(zf[:, sl], cc_ref[...])
            if radix == 1:
                ab_ref[0, :, sl] = cs[:, :FOURIER_HD].astype(BF16)
                ab_ref[1, :, sl] = cs[:, FOURIER_HD:].astype(BF16)
            else:
                ab_s[0, hh] = cs[:, :FOURIER_HD]
                ab_s[1, hh] = cs[:, FOURIER_HD:]
        z_vg = _dot(h, w_s[:, C_V:IN_COLS_PAD])
        for ref, bias, cols in ((laf_ref, bgf_ref, slice(0, GLA_DK_W)),
                                (lab_ref, bgb_ref, slice(GLA_DK_W, 2 * GLA_DK_W))):
            xb = xg[:, cols] + bias[...]
            ref[...] = (jnp.minimum(xb, 0.0) - jnp.log1p(jnp.exp(-jnp.abs(xb)))) * (1.0 / GATE_TEMP)
        q_ref[...] = z_qk[:, 0:GLA_DK_W] * (DK ** -0.5)
        k_ref[...] = z_qk[:, GLA_DK_W:]
        if radix > 1:
            for r in range(radix):
                for part in range(2):
                    for hh in range(FOURIER_HEADS):
                        c0 = r * FOURIER_W + hh * FOURIER_HD
                        ab_ref[part, :, c0:c0 + FOURIER_HD] = (
                            ab_s[part, hh, pl.ds(r, tm // radix, stride=radix), :].astype(BF16))
        v_ref[...] = z_vg[:, 0:GLA_DV_W].astype(BF16)
        sg_ref[...] = _silu(z_vg[:, GLA_DV_W:]).astype(BF16)

    pl.when(i == 0)(cast_weights)
    project_tile(norm_tile())
    for src, dst in zip(cast_in, cast_out):
        dst[...] = src[...].astype(BF16)


def _inproj(x, pe, mod_all, per_batch, layer, g, w_in, w_gate_f, b_gate_f, w_gate_b, b_gate_b, cc,
            seq_len, tm, cast=()):
    n = x.shape[0]
    n_tiles = n // tm
    radix = seq_len // _dft_len(seq_len)
    has_pe = pe is not None
    once = pl.Buffered(1)
    lay = lambda a: pl.BlockSpec((None,) + a.shape[1:], lambda i: (layer,) + (0,) * (a.ndim - 1),
                                 pipeline_mode=once)
    row = lambda w: pl.BlockSpec((tm, w), lambda i: (i, 0))
    in_specs = [row(D_MODEL)]
    args = [x]
    if has_pe:
        pr, pc = pe
        in_specs += [pl.BlockSpec((tm // GRID_W, pr.shape[1]), lambda i: (i % (seq_len // tm), 0)),
                     pl.BlockSpec(pc.shape, lambda i: (0, 0), pipeline_mode=once)]
        args += [pr, pc]
    weights = [mod_all, g, w_in, w_gate_f, w_gate_b, b_gate_f, b_gate_b]
    in_specs += [lay(a) for a in weights] + [pl.BlockSpec(cc.shape, lambda i: (0, 0),
                                                          pipeline_mode=once)]
    args += weights + [cc]
    assert all(a.shape[1] % n_tiles == 0 for a in cast)
    cast_rows = [a.shape[1] // n_tiles for a in cast]
    in_specs += [pl.BlockSpec((None, r, a.shape[2]), lambda i: (layer, i, 0))
                 for a, r in zip(cast, cast_rows)]
    args += list(cast)
    out_shape = (
        jax.ShapeDtypeStruct((2, n // radix, radix * FOURIER_W), BF16),
        jax.ShapeDtypeStruct((n, GLA_DK_W), F32),
        jax.ShapeDtypeStruct((n, GLA_DK_W), F32),
        jax.ShapeDtypeStruct((n, GLA_DV_W), BF16),
        jax.ShapeDtypeStruct((n, GLA_DK_W), F32),
        jax.ShapeDtypeStruct((n, GLA_DK_W), F32),
        jax.ShapeDtypeStruct((n, GLA_DV_W), BF16),
    )
    out_specs = (
        pl.BlockSpec((2, tm // radix, radix * FOURIER_W), lambda i: (0, i, 0)),
        row(GLA_DK_W), row(GLA_DK_W), row(GLA_DV_W), row(GLA_DK_W), row(GLA_DK_W), row(GLA_DV_W),
    )
    out_shape += tuple(jax.ShapeDtypeStruct((1,) + a.shape[1:], BF16) for a in cast)
    out_specs += tuple(pl.BlockSpec((None, r, a.shape[2]), lambda i: (0, i, 0))
                       for a, r in zip(cast, cast_rows))
    scratch = [pltpu.VMEM((D_MODEL, IN_COLS_PAD), BF16), pltpu.VMEM((LANE, 2 * GLA_DK_W), BF16)]
    if radix > 1:
        scratch.append(pltpu.VMEM((2, FOURIER_HEADS, tm, FOURIER_HD), F32))
    outs = pl.pallas_call(
        functools.partial(_inproj_kernel, has_pe=has_pe, per_batch=per_batch, radix=radix, tm=tm,
                          seq_len=seq_len, n_cast=len(cast)),
        grid=(n_tiles,),
        in_specs=in_specs,
        out_specs=out_specs,
        out_shape=out_shape,
        scratch_shapes=scratch,
        compiler_params=pltpu.CompilerParams(
            dimension_semantics=("arbitrary",), vmem_limit_bytes=VMEM_LIMIT),
        name="inproj",
    )(*args)
    return outs[:7], outs[7:]


DFT_MAX = 512


def _dft_len(seq_len):
    return min(seq_len, DFT_MAX)


def _seqdft_kernel(*refs, radix, nb):
    if radix > 1:
        c_ref, sn_ref, twc_ref, tws_ref, a_ref, b_ref, o_ref = refs
    else:
        c_ref, sn_ref, a_ref, b_ref, o_ref = refs
    c = c_ref[...]
    sn = sn_ref[...]
    for bi in range(nb):
        a = a_ref[bi]
        b = b_ref[bi]
        g_re = _dot(c, a) + _dot(sn, b)
        if radix == 1:
            o_ref[bi, 0] = g_re.astype(BF16)
            continue
        g_im = _dot(sn, a) - _dot(c, b)
        for k2 in range(radix):
            acc = g_re[:, 0:FOURIER_W]
            for r in range(1, radix):
                j = k2 * radix + r
                cols = slice(r * FOURIER_W, (r + 1) * FOURIER_W)
                acc = acc + twc_ref[:, j:j + 1] * g_re[:, cols] + tws_ref[:, j:j + 1] * g_im[:, cols]
            o_ref[bi, k2] = acc.astype(BF16)


def _seqdft(ab, tables, batch, seq_len, nb):
    p = _dft_len(seq_len)
    radix = seq_len // p
    width = radix * FOURIER_W
    ab4 = ab.reshape(2, batch, p, width)
    full = lambda a: pl.BlockSpec(a.shape, lambda i: (0,) * a.ndim)
    out = pl.pallas_call(
        functools.partial(_seqdft_kernel, radix=radix, nb=nb),
        grid=(batch // nb,),
        in_specs=[full(t) for t in tables] + [
            pl.BlockSpec((None, nb, p, width), lambda i: (0, i, 0, 0)),
            pl.BlockSpec((None, nb, p, width), lambda i: (1, i, 0, 0)),
        ],
        out_specs=pl.BlockSpec((nb, radix, p, FOURIER_W), lambda i: (i, 0, 0, 0)),
        out_shape=jax.ShapeDtypeStruct((batch, radix, p, FOURIER_W), BF16),
        compiler_params=pltpu.CompilerParams(
            dimension_semantics=("arbitrary",), vmem_limit_bytes=VMEM_LIMIT),
        name="seqdft",
    )(*tables, ab4, ab4)
    return out.reshape(batch * seq_len, FOURIER_W)


NEW_STATES = "new"


def _gla_kernel(*refs, seq_len, nb, has_s0, want_state, new_state_layer):
    if has_s0:
        s0_ref, *refs = refs
    if want_state and new_state_layer is None:
        _, *refs = refs
    q_ref, k_ref, v_ref, laf_ref, lab_ref, sg_ref, g_ref, og_ref, *refs = refs
    if want_state:
        sT_ref, *refs = refs
    qt_s, kt_s, ke_s, dec_s, o_s, st_s = refs
    nc = seq_len // CHUNK
    la_refs = (laf_ref, lab_ref)
    chains = [(bi, d) for bi in range(nb) for d in range(2)]

    ri = lax.broadcasted_iota(jnp.int32, (BLK, BLK), 0)
    ci = lax.broadcasted_iota(jnp.int32, (BLK, BLK), 1)
    same = (ri // CHUNK) == (ci // CHUNK)
    tri = (jnp.where(same & (ci <= ri), 1.0, 0.0).astype(BF16),
           jnp.where(same & (ci >= ri), 1.0, 0.0).astype(BF16))

    for bi, blk in [(bi, blk) for bi in range(nb) for blk in range(seq_len // BLK)]:
        rows = slice(blk * BLK, (blk + 1) * BLK)
        q = q_ref[bi, rows, :]
        k = k_ref[bi, rows, :]
        for d in range(2):
            la = la_refs[d][bi, rows, :]
            hi = la.astype(BF16)
            lo = (la - hi.astype(F32)).astype(BF16)
            b = _dot(tri[d], hi) + _dot(tri[d], lo)
            edge = CHUNK - 1 if d == 0 else 0
            tot = jnp.concatenate(
                [jnp.broadcast_to(b[cc * CHUNK + edge:cc * CHUNK + edge + 1, :], (CHUNK, GLA_DK_W))
                 for cc in range(BLK // CHUNK)], axis=0)
            qt_s[bi, d, rows, :] = (q * jnp.exp(b)).astype(BF16)
            kt_s[bi, d, rows, :] = (k * jnp.exp(-b)).astype(BF16)
            ke_s[bi, d, rows, :] = (k * jnp.exp(tot - b)).astype(BF16)
            for cc in range(BLK // CHUNK):
                c = blk * (BLK // CHUNK) + cc
                dec_s[bi, d, c:c + 1, :] = jnp.exp(tot[cc * CHUNK:cc * CHUNK + 1, :])
    for bi, d in chains:
        if has_s0:
            st_s[bi, d] = jnp.concatenate(
                [s0_ref[bi, d, h] for h in range(GLA_HEADS)], axis=0).T
        else:
            st_s[bi, d] = jnp.zeros((DV, GLA_DK_W), F32)

    lane_head = lax.broadcasted_iota(jnp.int32, (1, GLA_DK_W), 1) // DK
    head_mask = [jnp.where(lane_head == h, 1.0, 0.0).astype(BF16) for h in range(GLA_HEADS)]
    ti = lax.broadcasted_iota(jnp.int32, (CHUNK, CHUNK), 0)
    tj = lax.broadcasted_iota(jnp.int32, (CHUNK, CHUNK), 1)
    causal = (tj <= ti, tj >= ti)

    def chunk_rows(c):
        r0 = c * CHUNK
        return pl.ds(r0 if isinstance(r0, int) else pl.multiple_of(r0, CHUNK), CHUNK)

    def scan_step(step):
        chunk = (step, nc - 1 - step)
        rows = [chunk_rows(c) for c in chunk]
        v_c = {bi: [v_ref[bi, rows[d], :] for d in range(2)] for bi in range(nb)}
        res = {}
        for bi, d in chains:
            qt_c = qt_s[bi, d, rows[d], :]
            q_stack = jnp.concatenate([qt_c * head_mask[h] for h in range(GLA_HEADS)], axis=0)
            w = jnp.concatenate([st_s[bi, d].astype(BF16), kt_s[bi, d, rows[d], :]], axis=0)
            res[bi, d] = _dot_nt(q_stack, w)
        kv = {}
        for bi, d in chains:
            ke_c = ke_s[bi, d, rows[d], :]
            k_bd = jnp.concatenate([ke_c * head_mask[h] for h in range(GLA_HEADS)], axis=0)
            v_stack = jnp.concatenate(
                [v_c[bi][d][:, h * DV:(h + 1) * DV] for h in range(GLA_HEADS)], axis=0)
            v_stack_t = v_stack.astype(F32).T.astype(BF16)
            kv[bi, d] = _dot(v_stack_t, k_bd)
        for bi, d in chains:
            st_s[bi, d] = dec_s[bi, d, pl.ds(chunk[d], 1), :] * st_s[bi, d] + kv[bi, d]
        for bi, d in chains:
            for h in range(GLA_HEADS):
                hr = slice(h * CHUNK, (h + 1) * CHUNK)
                hv = slice(h * DV, (h + 1) * DV)
                att = jnp.where(causal[d], res[bi, d][hr, DV:DV + CHUNK], 0.0).astype(BF16)
                o_s[bi, d, rows[d], hv] = res[bi, d][hr, 0:DV] + _dot(att, v_c[bi][d][:, hv])

    if nc <= 4:
        for i in range(nc):
            scan_step(i)
    else:
        def body(i, carry):
            scan_step(i)
            return carry
        lax.fori_loop(0, nc, body, 0, unroll=4)

    g = g_ref[...]
    for bi in range(nb):
        for h in range(GLA_HEADS):
            hv = slice(h * DV, (h + 1) * DV)
            o = o_s[bi, 0, :, hv] + o_s[bi, 1, :, hv]
            og_ref[bi, :, hv] = (_rms(o, g) * sg_ref[bi, :, hv].astype(F32)).astype(BF16)
    if want_state:
        if new_state_layer is not None:
            own = sT_ref.at[:, new_state_layer]
            for l in range(DEPTH):
                if l != new_state_layer:
                    sT_ref[:, l] = jnp.zeros((nb, 2, GLA_HEADS, DK, DV), F32)
        else:
            own = sT_ref
        for bi, d in chains:
            s_d = st_s[bi, d].T
            for h in range(GLA_HEADS):
                own[bi, d, h] = s_d[h * DK:(h + 1) * DK, :]


def _gla(q, k, v, laf, lab, sg, g_gla, state_in, state_out, layer, batch, seq_len, nb):
    has_s0 = state_in is not None
    want_state = state_out is not None
    create = state_out is NEW_STATES
    r3 = lambda a: a.reshape(batch, seq_len, a.shape[-1])
    seq = lambda w: pl.BlockSpec((nb, seq_len, w), lambda b: (b, 0, 0))
    st_spec = pl.BlockSpec((nb, None, 2, GLA_HEADS, DK, DV), lambda b: (b, layer, 0, 0, 0, 0))
    in_specs = []
    args = []
    if has_s0:
        in_specs.append(st_spec)
        args.append(state_in)
    if want_state and not create:
        in_specs.append(pl.BlockSpec(memory_space=pl.ANY))
        args.append(state_out)
    in_specs += [seq(GLA_DK_W), seq(GLA_DK_W), seq(GLA_DV_W), seq(GLA_DK_W), seq(GLA_DK_W),
                 seq(GLA_DV_W), pl.BlockSpec((None, 1, DV), lambda b: (layer, 0, 0))]
    args += [r3(q), r3(k), r3(v), r3(laf), r3(lab), r3(sg), g_gla]
    out_specs = [seq(GLA_DV_W)]
    out_shape = [jax.ShapeDtypeStruct((batch, seq_len, GLA_DV_W), BF16)]
    if want_state:
        out_specs.append(
            pl.BlockSpec((nb, DEPTH, 2, GLA_HEADS, DK, DV), lambda b: (b, 0, 0, 0, 0, 0))
            if create else st_spec)
        out_shape.append(jax.ShapeDtypeStruct((batch, DEPTH, 2, GLA_HEADS, DK, DV), F32))
    res = pl.pallas_call(
        functools.partial(_gla_kernel, seq_len=seq_len, nb=nb, has_s0=has_s0,
                          want_state=want_state, new_state_layer=layer if create else None),
        grid=(batch // nb,),
        in_specs=in_specs,
        out_specs=out_specs,
        out_shape=out_shape,
        input_output_aliases={int(has_s0): 1} if want_state and not create else {},
        scratch_shapes=[
            pltpu.VMEM((nb, 2, seq_len, GLA_DK_W), BF16),
            pltpu.VMEM((nb, 2, seq_len, GLA_DK_W), BF16),
            pltpu.VMEM((nb, 2, seq_len, GLA_DK_W), BF16),
            pltpu.VMEM((nb, 2, seq_len // CHUNK, GLA_DK_W), F32),
            pltpu.VMEM((nb, 2, seq_len, GLA_DV_W), F32),
            pltpu.VMEM((nb, 2, DV, GLA_DK_W), F32),
        ],
        compiler_params=pltpu.CompilerParams(
            dimension_semantics=("arbitrary",), vmem_limit_bytes=VMEM_LIMIT),
        name="gla",
    )(*args)
    return res[0], (res[1] if want_state else None)


FF_CHUNK = 256


def _tail_kernel(*refs, has_pe, per_batch, seg, tm, seq_len, dft_here):
    if has_pe:
        x_ref, pr_ref, pc_ref, *refs = refs
    else:
        x_ref, *refs = refs
    if dft_here:
        dc_ref, dsn_ref, *refs = refs
    (yf_ref, og_ref, mod_ref, wo_ref, gpm_ref, gpf_ref, gqf_ref,
     wup_ref, cw_ref, cb_ref, wdn_ref, out_ref, x1_s, a_s) = refs
    i = pl.program_id(0)
    mod = mod_ref[pl.ds(_mod_row(i, tm, seq_len, per_batch), 1), :]

    if dft_here:
        yf = jnp.concatenate(
            [_dot(dc_ref[...], yf_ref[0, rows, :]) + _dot(dsn_ref[...], yf_ref[1, rows, :])
             for rows in (slice(r0, r0 + seq_len) for r0 in range(0, tm, seq_len))],
            axis=0).astype(BF16)
    else:
        yf = yf_ref[...]

    x = x_ref[...]
    if has_pe:
        x = _add_pos(x, pr_ref, pc_ref)
    y = _dot(yf, wo_ref[0:FOURIER_W, :]) + _dot(og_ref[...], wo_ref[FOURIER_W:, :])
    x1 = x + _rms(y, gpm_ref[...] * mod[:, 2 * D_MODEL:3 * D_MODEL])
    x1_s[...] = x1
    shift_f = mod[:, 3 * D_MODEL:4 * D_MODEL]
    gain_f = gpf_ref[...] * (1.0 + mod[:, 4 * D_MODEL:5 * D_MODEL])
    h = (_rms(x1, gain_f) + shift_f).astype(BF16)

    pos = lax.broadcasted_iota(jnp.int32, (tm, 1), 0) % seg
    first = pos == 0
    last = pos == seg - 1

    def conv_cols(c0, width):
        cols = slice(c0, c0 + width)
        u = _dot(h, wup_ref[:, cols])
        u_prev = jnp.where(first, 0.0, pltpu.roll(u, 1, 0))
        u_next = jnp.where(last, 0.0, pltpu.roll(u, tm - 1, 0))
        return (u_prev * cw_ref[0:1, cols] + u * cw_ref[1:2, cols]
                + u_next * cw_ref[2:3, cols] + cb_ref[:, cols])

    for c0 in range(0, D_FF, FF_CHUNK):
        width = min(FF_CHUNK, D_FF - c0)
        val = conv_cols(c0, width)
        gate = conv_cols(D_FF + c0, width)
        a_s[:, c0:c0 + width] = (_silu(gate) * val).astype(BF16)

    y2 = _dot(a_s[...], wdn_ref[...])
    out_ref[...] = x1_s[...] + _rms(y2, gqf_ref[...] * mod[:, 5 * D_MODEL:6 * D_MODEL])


def _tail(x, pe, yf, dft_tables, og, mod_all, per_batch, layer, ffn_w, gpm, gpf, gqf, cw, cb,
          seq_len, seg, tm):
    n = x.shape[0]
    n_tiles = n // tm
    has_pe = pe is not None
    dft_here = dft_tables is not None
    w_out, w_up, w_down = ffn_w
    row = lambda w: pl.BlockSpec((tm, w), lambda i: (i, 0))
    once = pl.Buffered(1)
    pick = lambda a, lead: pl.BlockSpec((None,) + a.shape[1:],
                                        lambda i: (lead,) + (0,) * (a.ndim - 1), pipeline_mode=once)
    lay = lambda a: pick(a, layer)
    own = lambda a: pick(a, 0)
    in_specs = [row(D_MODEL)]
    args = [x]
    if has_pe:
        pr, pc = pe
        in_specs += [pl.BlockSpec((tm // GRID_W, pr.shape[1]), lambda i: (i % (seq_len // tm), 0)),
                     pl.BlockSpec(pc.shape, lambda i: (0, 0), pipeline_mode=once)]
        args += [pr, pc]
    if dft_here:
        assert tm % seq_len == 0
        in_specs += [pl.BlockSpec(t.shape, lambda i: (0, 0), pipeline_mode=once)
                     for t in dft_tables]
        args += list(dft_tables)
        yf_spec = pl.BlockSpec((2, tm, FOURIER_W), lambda i: (0, i, 0))
    else:
        yf_spec = row(FOURIER_W)
    in_specs += [yf_spec, row(GLA_DV_W), lay(mod_all), own(w_out), lay(gpm), lay(gpf),
                 lay(gqf), own(w_up), lay(cw), lay(cb), own(w_down)]
    args += [yf, og, mod_all, w_out, gpm, gpf, gqf, w_up, cw, cb, w_down]
    return pl.pallas_call(
        functools.partial(_tail_kernel, has_pe=has_pe, per_batch=per_batch, seg=seg, tm=tm,
                          seq_len=seq_len, dft_here=dft_here),
        grid=(n_tiles,),
        in_specs=in_specs,
        out_specs=row(D_MODEL),
        out_shape=jax.ShapeDtypeStruct((n, D_MODEL), F32),
        scratch_shapes=[
            pltpu.VMEM((tm, D_MODEL), F32),
            pltpu.VMEM((tm, D_FF), BF16),
        ],
        compiler_params=pltpu.CompilerParams(
            dimension_semantics=("arbitrary",), vmem_limit_bytes=VMEM_LIMIT),
        name="tail",
    )(*args)


def _grid_pos_embed(rows, d):
    quarter = d // 4
    omega = 1.0 / (POS_BASE ** (jnp.arange(quarter, dtype=F32) / quarter))
    er = jnp.arange(rows, dtype=F32)[:, None] * omega
    ec = jnp.arange(GRID_W, dtype=F32)[:, None] * omega
    pr = jnp.concatenate([jnp.sin(er), jnp.cos(er)], axis=-1)
    pc = jnp.concatenate([jnp.sin(ec), jnp.cos(ec)], axis=-1)
    return pr, pc


def _dft_tables(n, scale):
    idx = jnp.arange(n, dtype=jnp.int32)
    ang = ((idx[:, None] * idx[None, :]) % n).astype(F32) * np.float32(2.0 * np.pi / n)
    return jnp.cos(ang) * np.float32(scale), jnp.sin(ang) * np.float32(scale)


def _seqdft_tables(seq_len):
    p = _dft_len(seq_len)
    radix = seq_len // p
    c, s = _dft_tables(p, seq_len ** -0.5)
    tables = (c.astype(BF16), (-s).astype(BF16))
    if radix > 1:
        k = (jnp.arange(p, dtype=jnp.int32)[:, None, None]
             + p * jnp.arange(radix, dtype=jnp.int32)[None, :, None])
        r = jnp.arange(radix, dtype=jnp.int32)[None, None, :]
        ang = ((k * r) % seq_len).astype(F32) * np.float32(2.0 * np.pi / seq_len)
        ang = ang.reshape(p, radix * radix)
        tables += (jnp.cos(ang), jnp.sin(ang))
    return tables


class _StreamCfg(NamedTuple):
    batch: int
    seq_len: int
    seg: int
    tm: int
    nb_seq: int
    dft_in_tail: bool


ROW_TILE = 512
SEQ_STEP_ROWS = 1024


def _stream_config(batch, seq_len, seg):
    tm = ROW_TILE
    return _StreamCfg(batch, seq_len, seg, tm=tm,
                      nb_seq=max(1, min(batch, SEQ_STEP_ROWS // seq_len)),
                      dft_in_tail=seq_len <= DFT_MAX and tm % seq_len == 0)


def _stream_layer(x, pe, mod_all, per_batch, state_in, state_out, layer, p, ffn_w, dft, cfg):
    cc, seq_tables = dft
    batch, seq_len = cfg.batch, cfg.seq_len
    cast = (p['w_out'], p['w_up'], p['w_down']) if ffn_w is None else ()
    (ab, q, k, v, laf, lab, sg), cast_out = _inproj(
        x, pe, mod_all, per_batch, layer, p['g_pre_mix'], p['w_in'], p['w_gate_f'], p['b_gate_f'],
        p['w_gate_b'], p['b_gate_b'], cc, seq_len, cfg.tm, cast)
    if ffn_w is None:
        ffn_w = cast_out
    if cfg.dft_in_tail:
        yf, tail_tables = ab, seq_tables
    else:
        yf, tail_tables = _seqdft(ab, seq_tables, batch, seq_len, cfg.nb_seq), None
    og, s_t = _gla(q, k, v, laf, lab, sg, p['g_gla'], state_in, state_out, layer, batch, seq_len,
                   cfg.nb_seq)
    x_new = _tail(x, pe, yf, tail_tables, og.reshape(batch * seq_len, GLA_DV_W), mod_all,
                  per_batch, layer, ffn_w, p['g_post_mix'], p['g_pre_ffn'], p['g_post_ffn'],
                  p['conv_w'], p['conv_b'], seq_len, cfg.seg, cfg.tm)
    return x_new, s_t, ffn_w


def kernel(x_prompt, x_sample, state_gla, c, c_ctx, g_pre_mix, g_post_mix, g_pre_ffn, g_post_ffn,
           w_ada, b_ada, w_in, w_gate_f, b_gate_f, w_gate_b, b_gate_b, g_gla, w_out, w_up, conv_w,
           conv_b, w_down):
    pb, pt, _ = x_prompt.shape
    sb, s_t, _ = x_sample.shape
    rows = s_t // GRID_W

    cv = jnp.concatenate([c_ctx[None], c, jnp.zeros((MOD_ROWS - 1 - sb, D_MODEL), F32)], axis=0)
    mod_all = _ada(cv, w_ada, b_ada)

    cc_c, cc_s = _dft_tables(FOURIER_HD, FOURIER_HD ** -0.5)
    cc = jnp.concatenate([cc_c, cc_s], axis=1).astype(BF16)
    dft_p = (cc, _seqdft_tables(pt))
    dft_s = (cc, _seqdft_tables(s_t))
    pe = _grid_pos_embed(rows, D_MODEL)

    xp = x_prompt.reshape(pb * pt, D_MODEL)
    xs = x_sample.reshape(sb * s_t, D_MODEL)
    new_state = NEW_STATES
    vec = lambda a: a[:, None]
    p = dict(
        g_pre_mix=vec(g_pre_mix), g_post_mix=vec(g_post_mix), g_pre_ffn=vec(g_pre_ffn),
        g_post_ffn=vec(g_post_ffn), g_gla=vec(g_gla), w_in=w_in, w_gate_f=w_gate_f,
        b_gate_f=vec(b_gate_f), w_gate_b=w_gate_b, b_gate_b=vec(b_gate_b),
        w_out=w_out, w_up=w_up, conv_w=conv_w, conv_b=vec(conv_b), w_down=w_down)
    cfg_p = _stream_config(pb, pt, pt)
    cfg_s = _stream_config(sb, s_t, GRID_W)
    for l in range(DEPTH):
        xp, new_state, ffn_w = _stream_layer(xp, None, mod_all, False, None, new_state, l, p,
                                             None, dft_p, cfg_p)
        xs, _, _ = _stream_layer(xs, pe if l == 0 else None, mod_all, True, state_gla, None, l, p,
                                 ffn_w, dft_s, cfg_s)
    return (xp.reshape(pb, pt, D_MODEL), xs.reshape(sb, s_t, D_MODEL), new_state)
```

```python
import functools
from typing import NamedTuple

import numpy as np
import jax
import jax.numpy as jnp
from jax import lax
from jax.experimental import pallas as pl
from jax.experimental.pallas import tpu as pltpu

D_MODEL = 1024
DEPTH = 2
GRID_W = 64
FOURIER_W = 512
FOURIER_HEADS = 4
FOURIER_HD = FOURIER_W // FOURIER_HEADS
GLA_DV_W = 512
GLA_DK_W = 256
GLA_HEADS = 4
DV = GLA_DV_W // GLA_HEADS
DK = GLA_DK_W // GLA_HEADS
GATE_RANK = 16
GATE_TEMP = 16.0
CHUNK = 64
D_FF = 11 * D_MODEL // 4
EPS = 1e-6
POS_BASE = 10000.0
S_F = FOURIER_W
S_Q = S_F + GLA_DK_W
S_K = S_Q + GLA_DK_W
S_V = S_K + GLA_DV_W
S_AF = S_V + GATE_RANK
S_AB = S_AF + GATE_RANK
IN_COLS = S_AB + GLA_DV_W

LANE = 128
C_A = 0
C_F = C_A + LANE
C_Q = C_F + FOURIER_W
C_K = C_Q + GLA_DK_W
C_V = C_K + GLA_DK_W
C_G = C_V + GLA_DV_W
IN_COLS_PAD = C_G + GLA_DV_W
MOD_ROWS = 8
BLK = 256

VMEM_LIMIT = 56 * 1024 * 1024

F32 = jnp.float32
BF16 = jnp.bfloat16


def _silu(x):
    return x * jax.nn.sigmoid(x)


def _rms(x, g):
    return x * lax.rsqrt(jnp.mean(x * x, axis=-1, keepdims=True) + EPS) * g


def _dot(a, b):
    return jnp.dot(a, b, preferred_element_type=F32)


def _dot_nt(a, b):
    return lax.dot_general(a, b, (((1,), (1,)), ((), ())), preferred_element_type=F32)


def _add_pos(x, pr_ref, pc_ref):
    half = D_MODEL // 2
    reps = x.shape[0] // GRID_W
    row_part = jnp.concatenate(
        [jnp.broadcast_to(pr_ref[j:j + 1, :], (GRID_W, half)) for j in range(reps)], axis=0)
    col_part = jnp.concatenate([pc_ref[...]] * reps, axis=0)
    return x + jnp.concatenate([row_part, col_part], axis=1)


def _ada_kernel(cv_ref, w_ref, b_ref, o_ref):
    s = _silu(cv_ref[...]).astype(BF16)
    o_ref[0] = _dot(s, w_ref[0].astype(BF16)) + b_ref[0]


ADA_COLS = 1536


def _ada(cv, w_ada, b_ada):
    tn = ADA_COLS
    n_out = w_ada.shape[-1]
    return pl.pallas_call(
        _ada_kernel,
        grid=(DEPTH, n_out // tn),
        in_specs=[
            pl.BlockSpec((MOD_ROWS, D_MODEL), lambda l, n: (0, 0)),
            pl.BlockSpec((1, D_MODEL, tn), lambda l, n: (l, 0, n)),
            pl.BlockSpec((1, 1, tn), lambda l, n: (l, 0, n)),
        ],
        out_specs=pl.BlockSpec((1, MOD_ROWS, tn), lambda l, n: (l, 0, n)),
        out_shape=jax.ShapeDtypeStruct((DEPTH, MOD_ROWS, n_out), F32),
        compiler_params=pltpu.CompilerParams(
            dimension_semantics=("arbitrary", "arbitrary"), vmem_limit_bytes=VMEM_LIMIT),
        name="ada",
    )(cv, w_ada, b_ada.reshape(DEPTH, 1, n_out))


def _mod_row(tile, tm, seq_len, per_batch):
    return 1 + (tile * tm) // seq_len if per_batch else 0


def _inproj_kernel(*refs, has_pe, per_batch, radix, tm, seq_len, n_cast):
    if has_pe:
        x_ref, pr_ref, pc_ref, *refs = refs
    else:
        x_ref, *refs = refs
    if radix > 1:
        *refs, ab_s = refs
    mod_ref, g_ref, w_ref, wgf_ref, wgb_ref, bgf_ref, bgb_ref, cc_ref, *refs = refs
    cast_in, refs = refs[:n_cast], refs[n_cast:]
    ab_ref, q_ref, k_ref, v_ref, laf_ref, lab_ref, sg_ref, *refs = refs
    cast_out, (w_s, wg_s) = refs[:n_cast], refs[n_cast:]
    i = pl.program_id(0)

    def cast_weights():
        rows_per = BLK
        for r0 in range(0, D_MODEL, rows_per):
            rs = slice(r0, r0 + rows_per)
            w_s[rs, C_A:C_F] = jnp.zeros((rows_per, LANE), BF16)
            w_s[rs, C_A:C_A + 2 * GATE_RANK] = w_ref[rs, S_V:S_AB].astype(BF16)
            w_s[rs, C_F:C_G] = w_ref[rs, 0:S_V].astype(BF16)
            w_s[rs, C_G:IN_COLS_PAD] = w_ref[rs, S_AB:IN_COLS].astype(BF16)
        wg_s[...] = jnp.zeros_like(wg_s)
        wg_s[0:GATE_RANK, 0:GLA_DK_W] = wgf_ref[...].astype(BF16)
        wg_s[GATE_RANK:2 * GATE_RANK, GLA_DK_W:] = wgb_ref[...].astype(BF16)

    def norm_tile():
        x = x_ref[...]
        if has_pe:
            x = _add_pos(x, pr_ref, pc_ref)
        mod = mod_ref[pl.ds(_mod_row(i, tm, seq_len, per_batch), 1), :]
        shift = mod[:, 0:D_MODEL]
        gain = g_ref[...] * (1.0 + mod[:, D_MODEL:2 * D_MODEL])
        return (_rms(x, gain) + shift).astype(BF16)

    def project_tile(h):
        z_af = _dot(h, w_s[:, C_A:C_Q])
        z_qk = _dot(h, w_s[:, C_Q:C_V])
        xg = _dot(z_af[:, C_A:C_F].astype(BF16), wg_s[...])
        zf = z_af[:, C_F:C_Q].astype(BF16)
        for hh in range(FOURIER_HEADS):
            sl = slice(hh * FOURIER_HD, (hh + 1) * FOURIER_HD)
            cs = _dot(zf[:, sl], cc_ref[...])
            if radix == 1:
                ab_ref[0, :, sl] = cs[:, :FOURIER_HD].astype(BF16)
                ab_ref[1, :, sl] = cs[:, FOURIER_HD:].astype(BF16)
            else:
                ab_s[0, hh] = cs[:, :FOURIER_HD]
                ab_s[1, hh] = cs[:, FOURIER_HD:]
        z_vg = _dot(h, w_s[:, C_V:IN_COLS_PAD])
        for ref, bias, cols in ((laf_ref, bgf_ref, slice(0, GLA_DK_W)),
                                (lab_ref, bgb_ref, slice(GLA_DK_W, 2 * GLA_DK_W))):
            xb = xg[:, cols] + bias[...]
            ref[...] = (jnp.minimum(xb, 0.0) - jnp.log1p(jnp.exp(-jnp.abs(xb)))) * (1.0 / GATE_TEMP)
        q_ref[...] = z_qk[:, 0:GLA_DK_W] * (DK ** -0.5)
        k_ref[...] = z_qk[:, GLA_DK_W:]
        if radix > 1:
            for r in range(radix):
                for part in range(2):
                    for hh in range(FOURIER_HEADS):
                        c0 = r * FOURIER_W + hh * FOURIER_HD
                        ab_ref[part, :, c0:c0 + FOURIER_HD] = (
                            ab_s[part, hh, pl.ds(r, tm // radix, stride=radix), :].astype(BF16))
        v_ref[...] = z_vg[:, 0:GLA_DV_W].astype(BF16)
        sg_ref[...] = _silu(z_vg[:, GLA_DV_W:]).astype(BF16)

    pl.when(i == 0)(cast_weights)
    project_tile(norm_tile())
    for src, dst in zip(cast_in, cast_out):
        dst[...] = src[...].astype(BF16)


def _inproj(x, pe, mod_all, per_batch, layer, g, w_in, w_gate_f, b_gate_f, w_gate_b, b_gate_b, cc,
            seq_len, tm, cast=()):
    n = x.shape[0]
    n_tiles = n // tm
    radix = seq_len // _dft_len(seq_len)
    has_pe = pe is not None
    once = pl.Buffered(1)
    lay = lambda a: pl.BlockSpec((None,) + a.shape[1:], lambda i: (layer,) + (0,) * (a.ndim - 1),
                                 pipeline_mode=once)
    row = lambda w: pl.BlockSpec((tm, w), lambda i: (i, 0))
    in_specs = [row(D_MODEL)]
    args = [x]
    if has_pe:
        pr, pc = pe
        in_specs += [pl.BlockSpec((tm // GRID_W, pr.shape[1]), lambda i: (i % (seq_len // tm), 0)),
                     pl.BlockSpec(pc.shape, lambda i: (0, 0), pipeline_mode=once)]
        args += [pr, pc]
    weights = [mod_all, g, w_in, w_gate_f, w_gate_b, b_gate_f, b_gate_b]
    in_specs += [lay(a) for a in weights] + [pl.BlockSpec(cc.shape, lambda i: (0, 0),
                                                          pipeline_mode=once)]
    args += weights + [cc]
    assert all(a.shape[1] % n_tiles == 0 for a in cast)
    cast_rows = [a.shape[1] // n_tiles for a in cast]
    in_specs += [pl.BlockSpec((None, r, a.shape[2]), lambda i: (layer, i, 0))
                 for a, r in zip(cast, cast_rows)]
    args += list(cast)
    out_shape = (
        jax.ShapeDtypeStruct((2, n // radix, radix * FOURIER_W), BF16),
        jax.ShapeDtypeStruct((n, GLA_DK_W), F32),
        jax.ShapeDtypeStruct((n, GLA_DK_W), F32),
        jax.ShapeDtypeStruct((n, GLA_DV_W), BF16),
        jax.ShapeDtypeStruct((n, GLA_DK_W), F32),
        jax.ShapeDtypeStruct((n, GLA_DK_W), F32),
        jax.ShapeDtypeStruct((n, GLA_DV_W), BF16),
    )
    out_specs = (
        pl.BlockSpec((2, tm // radix, radix * FOURIER_W), lambda i: (0, i, 0)),
        row(GLA_DK_W), row(GLA_DK_W), row(GLA_DV_W), row(GLA_DK_W), row(GLA_DK_W), row(GLA_DV_W),
    )
    out_shape += tuple(jax.ShapeDtypeStruct((1,) + a.shape[1:], BF16) for a in cast)
    out_specs += tuple(pl.BlockSpec((None, r, a.shape[2]), lambda i: (0, i, 0))
                       for a, r in zip(cast, cast_rows))
    scratch = [pltpu.VMEM((D_MODEL, IN_COLS_PAD), BF16), pltpu.VMEM((LANE, 2 * GLA_DK_W), BF16)]
    if radix > 1:
        scratch.append(pltpu.VMEM((2, FOURIER_HEADS, tm, FOURIER_HD), F32))
    outs = pl.pallas_call(
        functools.partial(_inproj_kernel, has_pe=has_pe, per_batch=per_batch, radix=radix, tm=tm,
                          seq_len=seq_len, n_cast=len(cast)),
        grid=(n_tiles,),
        in_specs=in_specs,
        out_specs=out_specs,
        out_shape=out_shape,
        scratch_shapes=scratch,
        compiler_params=pltpu.CompilerParams(
            dimension_semantics=("arbitrary",), vmem_limit_bytes=VMEM_LIMIT),
        name="inproj",
    )(*args)
    return outs[:7], outs[7:]


DFT_MAX = 512


def _dft_len(seq_len):
    return min(seq_len, DFT_MAX)


def _seqdft_kernel(*refs, radix, nb):
    if radix > 1:
        c_ref, sn_ref, twc_ref, tws_ref, a_ref, b_ref, o_ref = refs
    else:
        c_ref, sn_ref, a_ref, b_ref, o_ref = refs
    c = c_ref[...]
    sn = sn_ref[...]
    for bi in range(nb):
        a = a_ref[bi]
        b = b_ref[bi]
        g_re = _dot(c, a) + _dot(sn, b)
        if radix == 1:
            o_ref[bi, 0] = g_re.astype(BF16)
            continue
        g_im = _dot(sn, a) - _dot(c, b)
        for k2 in range(radix):
            acc = g_re[:, 0:FOURIER_W]
            for r in range(1, radix):
                j = k2 * radix + r
                cols = slice(r * FOURIER_W, (r + 1) * FOURIER_W)
                acc = acc + twc_ref[:, j:j + 1] * g_re[:, cols] + tws_ref[:, j:j + 1] * g_im[:, cols]
            o_ref[bi, k2] = acc.astype(BF16)


def _seqdft(ab, tables, batch, seq_len, nb):
    p = _dft_len(seq_len)
    radix = seq_len // p
    width = radix * FOURIER_W
    ab4 = ab.reshape(2, batch, p, width)
    full = lambda a: pl.BlockSpec(a.shape, lambda i: (0,) * a.ndim)
    out = pl.pallas_call(
        functools.partial(_seqdft_kernel, radix=radix, nb=nb),
        grid=(batch // nb,),
        in_specs=[full(t) for t in tables] + [
            pl.BlockSpec((None, nb, p, width), lambda i: (0, i, 0, 0)),
            pl.BlockSpec((None, nb, p, width), lambda i: (1, i, 0, 0)),
        ],
        out_specs=pl.BlockSpec((nb, radix, p, FOURIER_W), lambda i: (i, 0, 0, 0)),
        out_shape=jax.ShapeDtypeStruct((batch, radix, p, FOURIER_W), BF16),
        compiler_params=pltpu.CompilerParams(
            dimension_semantics=("arbitrary",), vmem_limit_bytes=VMEM_LIMIT),
        name="seqdft",
    )(*tables, ab4, ab4)
    return out.reshape(batch * seq_len, FOURIER_W)


NEW_STATES = "new"


def _gla_kernel(*refs, seq_len, nb, has_s0, want_state, new_state_layer):
    if has_s0:
        s0_ref, *refs = refs
    if want_state and new_state_layer is None:
        _, *refs = refs
    q_ref, k_ref, v_ref, laf_ref, lab_ref, sg_ref, g_ref, og_ref, *refs = refs
    if want_state:
        sT_ref, *refs = refs
    qt_s, kt_s, ke_s, dec_s, o_s, st_s = refs
    nc = seq_len // CHUNK
    la_refs = (laf_ref, lab_ref)
    chains = [(bi, d) for bi in range(nb) for d in range(2)]

    ri = lax.broadcasted_iota(jnp.int32, (BLK, BLK), 0)
    ci = lax.broadcasted_iota(jnp.int32, (BLK, BLK), 1)
    same = (ri // CHUNK) == (ci // CHUNK)
    tri = (jnp.where(same & (ci <= ri), 1.0, 0.0).astype(BF16),
           jnp.where(same & (ci >= ri), 1.0, 0.0).astype(BF16))

    for bi, blk in [(bi, blk) for bi in range(nb) for blk in range(seq_len // BLK)]:
        rows = slice(blk * BLK, (blk + 1) * BLK)
        q = q_ref[bi, rows, :]
        k = k_ref[bi, rows, :]
        for d in range(2):
            la = la_refs[d][bi, rows, :]
            hi = la.astype(BF16)
            lo = (la - hi.astype(F32)).astype(BF16)
            b = _dot(tri[d], hi) + _dot(tri[d], lo)
            edge = CHUNK - 1 if d == 0 else 0
            tot = jnp.concatenate(
                [jnp.broadcast_to(b[cc * CHUNK + edge:cc * CHUNK + edge + 1, :], (CHUNK, GLA_DK_W))
                 for cc in range(BLK // CHUNK)], axis=0)
            qt_s[bi, d, rows, :] = (q * jnp.exp(b)).astype(BF16)
            kt_s[bi, d, rows, :] = (k * jnp.exp(-b)).astype(BF16)
            ke_s[bi, d, rows, :] = (k * jnp.exp(tot - b)).astype(BF16)
            for cc in range(BLK // CHUNK):
                c = blk * (BLK // CHUNK) + cc
                dec_s[bi, d, c:c + 1, :] = jnp.exp(tot[cc * CHUNK:cc * CHUNK + 1, :])
    for bi, d in chains:
        if has_s0:
            st_s[bi, d] = jnp.concatenate(
                [s0_ref[bi, d, h] for h in range(GLA_HEADS)], axis=0).T
        else:
            st_s[bi, d] = jnp.zeros((DV, GLA_DK_W), F32)

    lane_head = lax.broadcasted_iota(jnp.int32, (1, GLA_DK_W), 1) // DK
    head_mask = [jnp.where(lane_head == h, 1.0, 0.0).astype(BF16) for h in range(GLA_HEADS)]
    ti = lax.broadcasted_iota(jnp.int32, (CHUNK, CHUNK), 0)
    tj = lax.broadcasted_iota(jnp.int32, (CHUNK, CHUNK), 1)
    causal = (tj <= ti, tj >= ti)

    def chunk_rows(c):
        r0 = c * CHUNK
        return pl.ds(r0 if isinstance(r0, int) else pl.multiple_of(r0, CHUNK), CHUNK)

    def scan_step(step):
        chunk = (step, nc - 1 - step)
        rows = [chunk_rows(c) for c in chunk]
        v_c = {bi: [v_ref[bi, rows[d], :] for d in range(2)] for bi in range(nb)}
        res = {}
        for bi, d in chains:
            qt_c = qt_s[bi, d, rows[d], :]
            q_stack = jnp.concatenate([qt_c * head_mask[h] for h in range(GLA_HEADS)], axis=0)
            w = jnp.concatenate([st_s[bi, d].astype(BF16), kt_s[bi, d, rows[d], :]], axis=0)
            res[bi, d] = _dot_nt(q_stack, w)
        kv = {}
        for bi, d in chains:
            ke_c = ke_s[bi, d, rows[d], :]
            k_bd = jnp.concatenate([ke_c * head_mask[h] for h in range(GLA_HEADS)], axis=0)
            v_stack = jnp.concatenate(
                [v_c[bi][d][:, h * DV:(h + 1) * DV] for h in range(GLA_HEADS)], axis=0)
            v_stack_t = v_stack.astype(F32).T.astype(BF16)
            kv[bi, d] = _dot(v_stack_t, k_bd)
        for bi, d in chains:
            st_s[bi, d] = dec_s[bi, d, pl.ds(chunk[d], 1), :] * st_s[bi, d] + kv[bi, d]
        for bi, d in chains:
            for h in range(GLA_HEADS):
                hr = slice(h * CHUNK, (h + 1) * CHUNK)
                hv = slice(h * DV, (h + 1) * DV)
                att = jnp.where(causal[d], res[bi, d][hr, DV:DV + CHUNK], 0.0).astype(BF16)
                o_s[bi, d, rows[d], hv] = res[bi, d][hr, 0:DV] + _dot(att, v_c[bi][d][:, hv])

    if nc <= 4:
        for i in range(nc):
            scan_step(i)
    else:
        def body(i, carry):
            scan_step(i)
            return carry
        lax.fori_loop(0, nc, body, 0, unroll=4)

    g = g_ref[...]
    for bi in range(nb):
        for h in range(GLA_HEADS):
            hv = slice(h * DV, (h + 1) * DV)
            o = o_s[bi, 0, :, hv] + o_s[bi, 1, :, hv]
            og_ref[bi, :, hv] = (_rms(o, g) * sg_ref[bi, :, hv].astype(F32)).astype(BF16)
    if want_state:
        if new_state_layer is not None:
            own = sT_ref.at[:, new_state_layer]
            for l in range(DEPTH):
                if l != new_state_layer:
                    sT_ref[:, l] = jnp.zeros((nb, 2, GLA_HEADS, DK, DV), F32)
        else:
            own = sT_ref
        for bi, d in chains:
            s_d = st_s[bi, d].T
            for h in range(GLA_HEADS):
                own[bi, d, h] = s_d[h * DK:(h + 1) * DK, :]


def _gla(q, k, v, laf, lab, sg, g_gla, state_in, state_out, layer, batch, seq_len, nb):
    has_s0 = state_in is not None
    want_state = state_out is not None
    create = state_out is NEW_STATES
    r3 = lambda a: a.reshape(batch, seq_len, a.shape[-1])
    seq = lambda w: pl.BlockSpec((nb, seq_len, w), lambda b: (b, 0, 0))
    st_spec = pl.BlockSpec((nb, None, 2, GLA_HEADS, DK, DV), lambda b: (b, layer, 0, 0, 0, 0))
    in_specs = []
    args = []
    if has_s0:
        in_specs.append(st_spec)
        args.append(state_in)
    if want_state and not create:
        in_specs.append(pl.BlockSpec(memory_space=pl.ANY))
        args.append(state_out)
    in_specs += [seq(GLA_DK_W), seq(GLA_DK_W), seq(GLA_DV_W), seq(GLA_DK_W), seq(GLA_DK_W),
                 seq(GLA_DV_W), pl.BlockSpec((None, 1, DV), lambda b: (layer, 0, 0))]
    args += [r3(q), r3(k), r3(v), r3(laf), r3(lab), r3(sg), g_gla]
    out_specs = [seq(GLA_DV_W)]
    out_shape = [jax.ShapeDtypeStruct((batch, seq_len, GLA_DV_W), BF16)]
    if want_state:
        out_specs.append(
            pl.BlockSpec((nb, DEPTH, 2, GLA_HEADS, DK, DV), lambda b: (b, 0, 0, 0, 0, 0))
            if create else st_spec)
        out_shape.append(jax.ShapeDtypeStruct((batch, DEPTH, 2, GLA_HEADS, DK, DV), F32))
    res = pl.pallas_call(
        functools.partial(_gla_kernel, seq_len=seq_len, nb=nb, has_s0=has_s0,
                          want_state=want_state, new_state_layer=layer if create else None),
        grid=(batch // nb,),
        in_specs=in_specs,
        out_specs=out_specs,
        out_shape=out_shape,
        input_output_aliases={int(has_s0): 1} if want_state and not create else {},
        scratch_shapes=[
            pltpu.VMEM((nb, 2, seq_len, GLA_DK_W), BF16),
            pltpu.VMEM((nb, 2, seq_len, GLA_DK_W), BF16),
            pltpu.VMEM((nb, 2, seq_len, GLA_DK_W), BF16),
            pltpu.VMEM((nb, 2, seq_len // CHUNK, GLA_DK_W), F32),
            pltpu.VMEM((nb, 2, seq_len, GLA_DV_W), F32),
            pltpu.VMEM((nb, 2, DV, GLA_DK_W), F32),
        ],
        compiler_params=pltpu.CompilerParams(
            dimension_semantics=("arbitrary",), vmem_limit_bytes=VMEM_LIMIT),
        name="gla",
    )(*args)
    return res[0], (res[1] if want_state else None)


FF_CHUNK = 256


def _tail_kernel(*refs, has_pe, per_batch, seg, tm, seq_len, dft_here):
    if has_pe:
        x_ref, pr_ref, pc_ref, *refs = refs
    else:
        x_ref, *refs = refs
    if dft_here:
        dc_ref, dsn_ref, *refs = refs
    (yf_ref, og_ref, mod_ref, wo_ref, gpm_ref, gpf_ref, gqf_ref,
     wup_ref, cw_ref, cb_ref, wdn_ref, out_ref, x1_s, a_s) = refs
    i = pl.program_id(0)
    mod = mod_ref[pl.ds(_mod_row(i, tm, seq_len, per_batch), 1), :]

    if dft_here:
        yf = jnp.concatenate(
            [_dot(dc_ref[...], yf_ref[0, rows, :]) + _dot(dsn_ref[...], yf_ref[1, rows, :])
             for rows in (slice(r0, r0 + seq_len) for r0 in range(0, tm, seq_len))],
            axis=0).astype(BF16)
    else:
        yf = yf_ref[...]

    x = x_ref[...]
    if has_pe:
        x = _add_pos(x, pr_ref, pc_ref)
    y = _dot(yf, wo_ref[0:FOURIER_W, :]) + _dot(og_ref[...], wo_ref[FOURIER_W:, :])
    x1 = x + _rms(y, gpm_ref[...] * mod[:, 2 * D_MODEL:3 * D_MODEL])
    x1_s[...] = x1
    shift_f = mod[:, 3 * D_MODEL:4 * D_MODEL]
    gain_f = gpf_ref[...] * (1.0 + mod[:, 4 * D_MODEL:5 * D_MODEL])
    h = (_rms(x1, gain_f) + shift_f).astype(BF16)

    pos = lax.broadcasted_iota(jnp.int32, (tm, 1), 0) % seg
    first = pos == 0
    last = pos == seg - 1

    def conv_cols(c0, width):
        cols = slice(c0, c0 + width)
        u = _dot(h, wup_ref[:, cols])
        u_prev = jnp.where(first, 0.0, pltpu.roll(u, 1, 0))
        u_next = jnp.where(last, 0.0, pltpu.roll(u, tm - 1, 0))
        return (u_prev * cw_ref[0:1, cols] + u * cw_ref[1:2, cols]
                + u_next * cw_ref[2:3, cols] + cb_ref[:, cols])

    for c0 in range(0, D_FF, FF_CHUNK):
        width = min(FF_CHUNK, D_FF - c0)
        val = conv_cols(c0, width)
        gate = conv_cols(D_FF + c0, width)
        a_s[:, c0:c0 + width] = (_silu(gate) * val).astype(BF16)

    y2 = _dot(a_s[...], wdn_ref[...])
    out_ref[...] = x1_s[...] + _rms(y2, gqf_ref[...] * mod[:, 5 * D_MODEL:6 * D_MODEL])


def _tail(x, pe, yf, dft_tables, og, mod_all, per_batch, layer, ffn_w, gpm, gpf, gqf, cw, cb,
          seq_len, seg, tm):
    n = x.shape[0]
    n_tiles = n // tm
    has_pe = pe is not None
    dft_here = dft_tables is not None
    w_out, w_up, w_down = ffn_w
    row = lambda w: pl.BlockSpec((tm, w), lambda i: (i, 0))
    once = pl.Buffered(1)
    pick = lambda a, lead: pl.BlockSpec((None,) + a.shape[1:],
                                        lambda i: (lead,) + (0,) * (a.ndim - 1), pipeline_mode=once)
    lay = lambda a: pick(a, layer)
    own = lambda a: pick(a, 0)
    in_specs = [row(D_MODEL)]
    args = [x]
    if has_pe:
        pr, pc = pe
        in_specs += [pl.BlockSpec((tm // GRID_W, pr.shape[1]), lambda i: (i % (seq_len // tm), 0)),
                     pl.BlockSpec(pc.shape, lambda i: (0, 0), pipeline_mode=once)]
        args += [pr, pc]
    if dft_here:
        assert tm % seq_len == 0
        in_specs += [pl.BlockSpec(t.shape, lambda i: (0, 0), pipeline_mode=once)
                     for t in dft_tables]
        args += list(dft_tables)
        yf_spec = pl.BlockSpec((2, tm, FOURIER_W), lambda i: (0, i, 0))
    else:
        yf_spec = row(FOURIER_W)
    in_specs += [yf_spec, row(GLA_DV_W), lay(mod_all), own(w_out), lay(gpm), lay(gpf),
                 lay(gqf), own(w_up), lay(cw), lay(cb), own(w_down)]
    args += [yf, og, mod_all, w_out, gpm, gpf, gqf, w_up, cw, cb, w_down]
    return pl.pallas_call(
        functools.partial(_tail_kernel, has_pe=has_pe, per_batch=per_batch, seg=seg, tm=tm,
                          seq_len=seq_len, dft_here=dft_here),
        grid=(n_tiles,),
        in_specs=in_specs,
        out_specs=row(D_MODEL),
        out_shape=jax.ShapeDtypeStruct((n, D_MODEL), F32),
        scratch_shapes=[
            pltpu.VMEM((tm, D_MODEL), F32),
            pltpu.VMEM((tm, D_FF), BF16),
        ],
        compiler_params=pltpu.CompilerParams(
            dimension_semantics=("arbitrary",), vmem_limit_bytes=VMEM_LIMIT),
        name="tail",
    )(*args)


def _grid_pos_embed(rows, d):
    quarter = d // 4
    omega = 1.0 / (POS_BASE ** (np.arange(quarter) / quarter))
    er = np.arange(rows)[:, None] * omega
    ec = np.arange(GRID_W)[:, None] * omega
    pr = np.concatenate([np.sin(er), np.cos(er)], axis=-1)
    pc = np.concatenate([np.sin(ec), np.cos(ec)], axis=-1)
    return jnp.asarray(pr, F32), jnp.asarray(pc, F32)


def _dft_tables(n, scale):
    idx = np.arange(n)
    ang = ((idx[:, None] * idx[None, :]) % n) * (2.0 * np.pi / n)
    return (np.cos(ang) * scale).astype(np.float32), (np.sin(ang) * scale).astype(np.float32)


def _seqdft_tables(seq_len):
    p = _dft_len(seq_len)
    radix = seq_len // p
    c, s = _dft_tables(p, seq_len ** -0.5)
    tables = (jnp.asarray(c).astype(BF16), jnp.asarray(-s).astype(BF16))
    if radix > 1:
        k = np.arange(p)[:, None, None] + p * np.arange(radix)[None, :, None]
        r = np.arange(radix)[None, None, :]
        ang = (((k * r) % seq_len) * (2.0 * np.pi / seq_len)).reshape(p, radix * radix)
        tables += (jnp.asarray(np.cos(ang), F32), jnp.asarray(np.sin(ang), F32))
    return tables


class _StreamCfg(NamedTuple):
    batch: int
    seq_len: int
    seg: int
    tm: int
    nb_seq: int
    dft_in_tail: bool


ROW_TILE = 512
SEQ_STEP_ROWS = 1024


def _stream_config(batch, seq_len, seg):
    tm = ROW_TILE
    return _StreamCfg(batch, seq_len, seg, tm=tm,
                      nb_seq=max(1, min(batch, SEQ_STEP_ROWS // seq_len)),
                      dft_in_tail=seq_len <= DFT_MAX and tm % seq_len == 0)


def _stream_layer(x, pe, mod_all, per_batch, state_in, state_out, layer, p, ffn_w, dft, cfg):
    cc, seq_tables = dft
    batch, seq_len = cfg.batch, cfg.seq_len
    cast = (p['w_out'], p['w_up'], p['w_down']) if ffn_w is None else ()
    (ab, q, k, v, laf, lab, sg), cast_out = _inproj(
        x, pe, mod_all, per_batch, layer, p['g_pre_mix'], p['w_in'], p['w_gate_f'], p['b_gate_f'],
        p['w_gate_b'], p['b_gate_b'], cc, seq_len, cfg.tm, cast)
    if ffn_w is None:
        ffn_w = cast_out
    if cfg.dft_in_tail:
        yf, tail_tables = ab, seq_tables
    else:
        yf, tail_tables = _seqdft(ab, seq_tables, batch, seq_len, cfg.nb_seq), None
    og, s_t = _gla(q, k, v, laf, lab, sg, p['g_gla'], state_in, state_out, layer, batch, seq_len,
                   cfg.nb_seq)
    x_new = _tail(x, pe, yf, tail_tables, og.reshape(batch * seq_len, GLA_DV_W), mod_all,
                  per_batch, layer, ffn_w, p['g_post_mix'], p['g_pre_ffn'], p['g_post_ffn'],
                  p['conv_w'], p['conv_b'], seq_len, cfg.seg, cfg.tm)
    return x_new, s_t, ffn_w


def kernel(x_prompt, x_sample, state_gla, c, c_ctx, g_pre_mix, g_post_mix, g_pre_ffn, g_post_ffn,
           w_ada, b_ada, w_in, w_gate_f, b_gate_f, w_gate_b, b_gate_b, g_gla, w_out, w_up, conv_w,
           conv_b, w_down):
    pb, pt, _ = x_prompt.shape
    sb, s_t, _ = x_sample.shape
    rows = s_t // GRID_W

    cv = jnp.concatenate([c_ctx[None], c, jnp.zeros((MOD_ROWS - 1 - sb, D_MODEL), F32)], axis=0)
    mod_all = _ada(cv, w_ada, b_ada)

    cc_c, cc_s = _dft_tables(FOURIER_HD, FOURIER_HD ** -0.5)
    cc = jnp.asarray(np.concatenate([cc_c, cc_s], axis=1)).astype(BF16)
    dft_p = (cc, _seqdft_tables(pt))
    dft_s = (cc, _seqdft_tables(s_t))
    pe = _grid_pos_embed(rows, D_MODEL)

    xp = x_prompt.reshape(pb * pt, D_MODEL)
    xs = x_sample.reshape(sb * s_t, D_MODEL)
    new_state = NEW_STATES
    vec = lambda a: a[:, None]
    p = dict(
        g_pre_mix=vec(g_pre_mix), g_post_mix=vec(g_post_mix), g_pre_ffn=vec(g_pre_ffn),
        g_post_ffn=vec(g_post_ffn), g_gla=vec(g_gla), w_in=w_in, w_gate_f=w_gate_f,
        b_gate_f=vec(b_gate_f), w_gate_b=w_gate_b, b_gate_b=vec(b_gate_b),
        w_out=w_out, w_up=w_up, conv_w=conv_w, conv_b=vec(conv_b), w_down=w_down)
    cfg_p = _stream_config(pb, pt, pt)
    cfg_s = _stream_config(sb, s_t, GRID_W)
    for l in range(DEPTH):
        xp, new_state, ffn_w = _stream_layer(xp, None, mod_all, False, None, new_state, l, p,
                                             None, dft_p, cfg_p)
        xs, _, _ = _stream_layer(xs, pe if l == 0 else None, mod_all, True, state_gla, None, l, p,
                                 ffn_w, dft_s, cfg_s)
    return (xp.reshape(pb, pt, D_MODEL), xs.reshape(sb, s_t, D_MODEL), new_state)
```
